```python
import math
import jax, jax.numpy as jnp
from jax import lax
import numpy as np

D_MODEL = 1024
BATCH = 2
SEQ = 8192
DEPTH = 4
DEC_BATCH = 4
DEC_SEQ = 4096
PAST_LEN = 128

GRID_W = 64
HEAD_DIM = 64
N_SELF_HEADS = 12
N_KV_HEADS = 4
N_MEM_HEADS = 4
N_MEM = 256
SELF_W = N_SELF_HEADS * HEAD_DIM
KV_W_A = N_KV_HEADS * HEAD_DIM
MEM_W = N_MEM_HEADS * HEAD_DIM
MIX_W = SELF_W + MEM_W
IN_W_A = SELF_W + 2 * KV_W_A + MEM_W
IN_W_B = 3 * SELF_W + MEM_W
Q_BLOCK = 128
MAX_ROW_WIN = 8
COL_WIN = 16
ROPE_THETA = 10000.0
ROPE_AXIS_DIM = HEAD_DIM // 2
N_EXPERTS = 16
EC_CAPACITY_FACTOR = 2
D_EXPERT = 2 * D_MODEL
N_LAYERS_A = (DEPTH + 1) // 2
N_LAYERS_B = DEPTH // 2
EPS = 1e-6

kernel_name = "hybrid_gqa_natten_ec_moe_encoder"


def rms_norm(x, g):
    xf = x.astype(jnp.float32)
    y = xf * lax.rsqrt(jnp.mean(xf * xf, axis=-1, keepdims=True) + EPS)
    return (y * g.astype(jnp.float32)).astype(x.dtype)


def axial_rope_tables(S):
    t = jnp.arange(S)
    row = (t // GRID_W).astype(jnp.float32)
    col = (t % GRID_W).astype(jnp.float32)
    inv = 1.0 / (ROPE_THETA ** (jnp.arange(0, ROPE_AXIS_DIM, 2, dtype=jnp.float32) / ROPE_AXIS_DIM))
    ar = row[:, None] * inv[None, :]
    ac = col[:, None] * inv[None, :]
    return (jnp.cos(ar), jnp.sin(ar), jnp.cos(ac), jnp.sin(ac))


def _rotate_half(x, cos, sin):
    half = x.shape[-1] // 2
    x1, x2 = x[..., :half], x[..., half:]
    c = cos[None, :, None, :]
    s = sin[None, :, None, :]
    return jnp.concatenate([x1 * c - x2 * s, x2 * c + x1 * s], axis=-1)


def apply_axial_rope(x, tables):
    cr, sr, cc, sc = tables
    xf = x.astype(jnp.float32)
    out = jnp.concatenate([_rotate_half(xf[..., :ROPE_AXIS_DIM], cr, sr),
                           _rotate_half(xf[..., ROPE_AXIS_DIM:], cc, sc)], axis=-1)
    return out.astype(x.dtype)


def gqa_blocked(q, k, v):
    B, S, Hq, dh = q.shape
    G = Hq // N_KV_HEADS
    nb = S // Q_BLOCK
    qb = q.reshape(B, nb, Q_BLOCK, N_KV_HEADS, G, dh).transpose(1, 0, 2, 3, 4, 5)
    scale = dh ** -0.5

    def block(qblk):
        s = jnp.einsum('bqgrd,bkgd->bgrqk', qblk, k).astype(jnp.float32) * scale
        p = jax.nn.softmax(s, axis=-1).astype(v.dtype)
        return jnp.einsum('bgrqk,bkgd->bqgrd', p, v)

    o = lax.map(block, qb)
    return o.transpose(1, 0, 2, 3, 4, 5).reshape(B, S, Hq * dh)


def neighbourhood_attention(q, k, v, rpb):
    B, S, H, dh = q.shape
    rows = S // GRID_W
    wr = min(MAX_ROW_WIN, rows)
    qg = q.reshape(B, rows, GRID_W, H, dh).transpose(1, 0, 2, 3, 4)
    kg = k.reshape(B, rows, GRID_W, H, dh)
    vg = v.reshape(B, rows, GRID_W, H, dh)
    cols = jnp.arange(GRID_W)
    col_start = jnp.clip(cols - COL_WIN // 2, 0, GRID_W - COL_WIN)
    col_idx = col_start[:, None] + jnp.arange(COL_WIN)[None, :]
    dc = col_idx - cols[:, None] + (COL_WIN - 1)
    scale = dh ** -0.5

    def row_block(args):
        r, q_row = args
        sr = jnp.clip(r - wr // 2, 0, rows - wr)
        k_rows = lax.dynamic_slice_in_dim(kg, sr, wr, axis=1)
        v_rows = lax.dynamic_slice_in_dim(vg, sr, wr, axis=1)
        k_nb = k_rows[:, :, col_idx]
        v_nb = v_rows[:, :, col_idx]
        dr = sr + jnp.arange(wr) - r + (MAX_ROW_WIN - 1)
        bias = rpb[:, dr][:, :, dc]
        bias = bias.transpose(0, 2, 1, 3)[None].astype(jnp.float32)
        s = jnp.einsum('bchd,bwcjhd->bhcwj', q_row, k_nb).astype(jnp.float32) * scale + bias
        p = jax.nn.softmax(s.reshape(B, H, GRID_W, wr * COL_WIN), axis=-1)
        p = p.reshape(s.shape).astype(v.dtype)
        return jnp.einsum('bhcwj,bwcjhd->bchd', p, v_nb)

    o = lax.map(row_block, (jnp.arange(rows), qg))
    return o.transpose(1, 0, 2, 3, 4).reshape(B, S, H * dh)


def memory_attention(qx, mem_n, w_kv, q_g, k_g):
    B, S, _ = qx.shape
    M = mem_n.shape[1]
    q = rms_norm(qx.reshape(B, S, N_MEM_HEADS, HEAD_DIM), q_g)
    kv = mem_n @ w_kv
    km = rms_norm(kv[..., :MEM_W].reshape(B, M, N_MEM_HEADS, HEAD_DIM), k_g)
    vm = kv[..., MEM_W:].reshape(B, M, N_MEM_HEADS, HEAD_DIM)
    s = jnp.einsum('bshd,bmhd->bhsm', q, km).astype(jnp.float32) * (HEAD_DIM ** -0.5)
    p = jax.nn.softmax(s, axis=-1).astype(vm.dtype)
    return jnp.einsum('bhsm,bmhd->bshd', p, vm).reshape(B, S, MEM_W)


def expert_choice_moe(h, w_router, w_gate, w_up, w_down):
    B, S, D = h.shape
    N = B * S
    C = EC_CAPACITY_FACTOR * N // N_EXPERTS
    xs = h.reshape(N, D)
    aff = jax.nn.softmax((xs @ w_router).astype(jnp.float32), axis=-1)
    gates, idx = lax.top_k(aff.T, C)
    xe = xs[idx]
    hid = jax.nn.silu(jnp.einsum('ecd,edf->ecf', xe, w_gate)) * jnp.einsum('ecd,edf->ecf', xe, w_up)
    ye = jnp.einsum('ecf,efd->ecd', hid, w_down) * gates[..., None].astype(h.dtype)
    y = jnp.zeros_like(xs).at[idx.reshape(-1)].add(ye.reshape(-1, D))
    return y.reshape(B, S, D)


def trunk(x, mem, norm_mix, w_in_a, q_norm_a, k_norm_a, w_in_b, q_norm_b, k_norm_b, na_rpb,
          norm_mem, w_mem_kv, xq_norm, xk_norm, w_o, norm_ffn, w_router, w_gate, w_up, w_down):
    B, S, _ = x.shape
    tables = axial_rope_tables(S)
    for i in range(DEPTH):
        j = i // 2
        h = rms_norm(x, norm_mix[i])
        if i % 2 == 0:
            proj = h @ w_in_a[j]
            q = proj[..., :SELF_W].reshape(B, S, N_SELF_HEADS, HEAD_DIM)
            k = proj[..., SELF_W:SELF_W + KV_W_A].reshape(B, S, N_KV_HEADS, HEAD_DIM)
            v = proj[..., SELF_W + KV_W_A:SELF_W + 2 * KV_W_A].reshape(B, S, N_KV_HEADS, HEAD_DIM)
            qx = proj[..., SELF_W + 2 * KV_W_A:]
            q = apply_axial_rope(rms_norm(q, q_norm_a[j]), tables)
            k = apply_axial_rope(rms_norm(k, k_norm_a[j]), tables)
            self_out = gqa_blocked(q, k, v)
        else:
            proj = h @ w_in_b[j]
            q = proj[..., :SELF_W].reshape(B, S, N_SELF_HEADS, HEAD_DIM)
            k = proj[..., SELF_W:2 * SELF_W].reshape(B, S, N_SELF_HEADS, HEAD_DIM)
            v = proj[..., 2 * SELF_W:3 * SELF_W].reshape(B, S, N_SELF_HEADS, HEAD_DIM)
            qx = proj[..., 3 * SELF_W:]
            q = rms_norm(q, q_norm_b[j])
            k = rms_norm(k, k_norm_b[j])
            self_out = neighbourhood_attention(q, k, v, na_rpb[j])
        mem_out = memory_attention(qx, rms_norm(mem, norm_mem[i]), w_mem_kv[i], xq_norm[i], xk_norm[i])
        x = x + jnp.concatenate([self_out, mem_out], axis=-1) @ w_o[i]
        x = x + expert_choice_moe(rms_norm(x, norm_ffn[i]), w_router[i], w_gate[i], w_up[i], w_down[i])
    return x


def setup_inputs(seed: int = 0) -> dict:
    key = jax.random.key(seed)
    ks = jax.random.split(key, 24)
    f32 = jnp.float32
    D = D_MODEL

    def nrm(k, shape, scale):
        return jax.random.normal(k, shape, f32) * scale

    def gain(k, shape):
        return 1.0 + 0.02 * jax.random.normal(k, shape, f32)

    return {
        "x_prompt": nrm(ks[0], (BATCH, SEQ, D), 1.0),
        "x_sample": nrm(ks[1], (DEC_BATCH, DEC_SEQ, D), 1.0),
        "mem_prompt": nrm(ks[2], (BATCH, N_MEM, D), 1.0),
        "mem_sample": nrm(ks[3], (DEC_BATCH, N_MEM, D), 1.0),
        "norm_mix": gain(ks[4], (DEPTH, D)),
        "w_in_a": nrm(ks[5], (N_LAYERS_A, D, IN_W_A), D ** -0.5),
        "q_norm_a": gain(ks[6], (N_LAYERS_A, HEAD_DIM)),
        "k_norm_a": gain(ks[7], (N_LAYERS_A, HEAD_DIM)),
        "w_in_b": nrm(ks[8], (N_LAYERS_B, D, IN_W_B), D ** -0.5),
        "q_norm_b": gain(ks[9], (N_LAYERS_B, HEAD_DIM)),
        "k_norm_b": gain(ks[10], (N_LAYERS_B, HEAD_DIM)),
        "na_rpb": nrm(ks[11], (N_LAYERS_B, N_SELF_HEADS, 2 * MAX_ROW_WIN - 1, 2 * COL_WIN - 1), 0.1),
        "norm_mem": gain(ks[12], (DEPTH, D)),
        "w_mem_kv": nrm(ks[13], (DEPTH, D, 2 * MEM_W), D ** -0.5),
        "xq_norm": gain(ks[14], (DEPTH, HEAD_DIM)),
        "xk_norm": gain(ks[15], (DEPTH, HEAD_DIM)),
        "w_o": nrm(ks[16], (DEPTH, MIX_W, D), MIX_W ** -0.5),
        "norm_ffn": gain(ks[17], (DEPTH, D)),
        "w_router": nrm(ks[18], (DEPTH, D, N_EXPERTS), D ** -0.5),
        "w_gate": nrm(ks[19], (DEPTH, N_EXPERTS, D, D_EXPERT), D ** -0.5),
        "w_up": nrm(ks[20], (DEPTH, N_EXPERTS, D, D_EXPERT), D ** -0.5),
        "w_down": nrm(ks[21], (DEPTH, N_EXPERTS, D_EXPERT, D), D_EXPERT ** -0.5),
    }


def reference(x_prompt, x_sample, mem_prompt, mem_sample, norm_mix, w_in_a, q_norm_a, k_norm_a,
              w_in_b, q_norm_b, k_norm_b, na_rpb, norm_mem, w_mem_kv, xq_norm, xk_norm, w_o,
              norm_ffn, w_router, w_gate, w_up, w_down):
    y_prompt = trunk(x_prompt, mem_prompt, norm_mix, w_in_a, q_norm_a, k_norm_a, w_in_b, q_norm_b,
                     k_norm_b, na_rpb, norm_mem, w_mem_kv, xq_norm, xk_norm, w_o, norm_ffn,
                     w_router, w_gate, w_up, w_down)
    y_sample = trunk(x_sample, mem_sample, norm_mix, w_in_a, q_norm_a, k_norm_a, w_in_b, q_norm_b,
                     k_norm_b, na_rpb, norm_mem, w_mem_kv, xq_norm, xk_norm, w_o, norm_ffn,
                     w_router, w_gate, w_up, w_down)
    return (y_prompt, y_sample)
```

```python
import functools
import math

import jax
import jax.numpy as jnp
import numpy as np
from jax import lax
from jax.experimental import pallas as pl
from jax.experimental.pallas import tpu as pltpu

F32 = jnp.float32
BF16 = jnp.bfloat16

D_MODEL = 1024
DEPTH = 4
GRID_W = 64
HEAD_DIM = 64
N_SELF_HEADS = 12
N_KV_HEADS = 4
N_MEM_HEADS = 4
N_MEM = 256
SELF_W = N_SELF_HEADS * HEAD_DIM
KV_W_A = N_KV_HEADS * HEAD_DIM
MEM_W = N_MEM_HEADS * HEAD_DIM
IN_W_A = SELF_W + 2 * KV_W_A + MEM_W
IN_W_B = 3 * SELF_W + MEM_W
ROW_WIN = 8
COL_WIN = 16
ROPE_THETA = 10000.0
ROPE_AXIS_DIM = HEAD_DIM // 2
N_EXPERTS = 16
EC_CAPACITY_FACTOR = 2
D_EXPERT = 2 * D_MODEL
EPS = 1e-6

LANES = 128
VMEM_LIMIT = 56 * 1024 * 1024
NEG_BIG = -1e30


def _cparams(sem):
    return pltpu.CompilerParams(dimension_semantics=sem, vmem_limit_bytes=VMEM_LIMIT)


def _swap16(y):
    lane = lax.broadcasted_iota(jnp.int32, y.shape, 1)
    up = pltpu.roll(y, 16, axis=1)
    dn = pltpu.roll(y, LANES - 16, axis=1)
    return jnp.where((lane & 16) != 0, up, dn)


def _proj_kernel(x_ref, g_ref, w_ref, gain_ref, cos_ref, sin_ref, bd_ref, o_ref, *,
                 n_norm_slabs_mask, n_rope_slabs):
    x = x_ref[...]
    ms = jnp.mean(x * x, axis=-1, keepdims=True)
    h = (x * lax.rsqrt(ms + EPS)) * g_ref[...]
    proj = jnp.dot(h.astype(BF16), w_ref[...], preferred_element_type=F32)
    n_slabs = proj.shape[1] // LANES
    bd = bd_ref[...]
    for s in range(n_slabs):
        y = proj[:, s * LANES:(s + 1) * LANES]
        if n_norm_slabs_mask[s]:
            sq = y * y
            hi = sq.astype(BF16)
            lo = (sq - hi.astype(F32)).astype(BF16)
            ss = (jnp.dot(hi, bd, preferred_element_type=F32)
                  + jnp.dot(lo, bd, preferred_element_type=F32))
            y = (y * lax.rsqrt(ss * (1.0 / HEAD_DIM) + EPS)) * gain_ref[:, s * LANES:(s + 1) * LANES]
        if s < n_rope_slabs:
            y = y * cos_ref[...] + _swap16(y) * sin_ref[...]
        o_ref[:, s * LANES:(s + 1) * LANES] = y.astype(o_ref.dtype)


def _fused_proj(x2d, g, w_bf16, gain_full, norm_mask, n_rope_slabs, cos_t, sin_t, seq, tm=256):
    n, d = x2d.shape
    w_out = w_bf16.shape[1]
    assert n % tm == 0 and seq % tm == 0
    blocks_per_seq = seq // tm
    bd = jnp.asarray(np.kron(np.eye(LANES // HEAD_DIM), np.ones((HEAD_DIM, HEAD_DIM))), BF16)
    kern = functools.partial(_proj_kernel, n_norm_slabs_mask=tuple(norm_mask),
                             n_rope_slabs=n_rope_slabs)
    return pl.pallas_call(
        kern,
        grid=(n // tm,),
        in_specs=[
            pl.BlockSpec((tm, d), lambda i: (i, 0)),
            pl.BlockSpec((1, d), lambda i: (0, 0)),
            pl.BlockSpec((d, w_out), lambda i: (0, 0)),
            pl.BlockSpec((1, w_out), lambda i: (0, 0)),
            pl.BlockSpec((tm, LANES), lambda i: (i % blocks_per_seq, 0)),
            pl.BlockSpec((tm, LANES), lambda i: (i % blocks_per_seq, 0)),
            pl.BlockSpec((LANES, LANES), lambda i: (0, 0)),
        ],
        out_specs=pl.BlockSpec((tm, w_out), lambda i: (i, 0)),
        out_shape=jax.ShapeDtypeStruct((n, w_out), BF16),
        compiler_params=_cparams(("parallel",)),
        name="fused_proj",
    )(x2d, g.reshape(1, d), w_bf16, gain_full.reshape(1, w_out), cos_t, sin_t, bd)


def _rope_tables(seq):
    t = jnp.arange(seq)
    row = (t // GRID_W).astype(F32)
    col = (t % GRID_W).astype(F32)
    inv = 1.0 / (ROPE_THETA ** (jnp.arange(0, ROPE_AXIS_DIM, 2, dtype=F32) / ROPE_AXIS_DIM))
    ar = row[:, None] * inv[None, :]
    ac = col[:, None] * inv[None, :]
    cr, sr, cc, sc = jnp.cos(ar), jnp.sin(ar), jnp.cos(ac), jnp.sin(ac)
    cos_h = jnp.concatenate([cr, cr, cc, cc], axis=-1)
    sin_h = jnp.concatenate([-sr, sr, -sc, sc], axis=-1)
    reps = LANES // HEAD_DIM
    return jnp.tile(cos_h, (1, reps)), jnp.tile(sin_h, (1, reps))


def _gqa_kernel(q_ref, k_ref, v_ref, o_ref, *, tk):
    tq = q_ref.shape[0]
    seq = k_ref.shape[0]
    group = N_SELF_HEADS // N_KV_HEADS
    n_kv = seq // tk
    for g in range(N_KV_HEADS):
        qs = jnp.concatenate(
            [q_ref[:, (g * group + r) * HEAD_DIM:(g * group + r + 1) * HEAD_DIM] for r in range(group)],
            axis=0)
        m_rows = group * tq

        def body(j, carry, g=g, qs=qs):
            m_i, l_i, acc = carry
            start = pl.multiple_of(j * tk, tk)
            k_t = k_ref[pl.ds(start, tk), g * HEAD_DIM:(g + 1) * HEAD_DIM]
            v_t = v_ref[pl.ds(start, tk), g * HEAD_DIM:(g + 1) * HEAD_DIM]
            s = lax.dot_general(qs, k_t, (((1,), (1,)), ((), ())), preferred_element_type=F32)
            m_new = jnp.maximum(m_i, jnp.max(s, axis=-1, keepdims=True))
            alpha = jnp.exp(m_i - m_new)
            p = jnp.exp(s - m_new)
            l_new = alpha * l_i + jnp.sum(p, axis=-1, keepdims=True)
            acc_new = alpha * acc + jnp.dot(p.astype(BF16), v_t, preferred_element_type=F32)
            return m_new, l_new, acc_new

        init = (jnp.full((m_rows, 1), NEG_BIG, F32), jnp.zeros((m_rows, 1), F32),
                jnp.zeros((m_rows, HEAD_DIM), F32))
        _, l_f, acc_f = lax.fori_loop(0, n_kv, body, init)
        o = acc_f / l_f
        for r in range(group):
            hh = g * group + r
            o_ref[:, hh * HEAD_DIM:(hh + 1) * HEAD_DIM] = o[r * tq:(r + 1) * tq].astype(o_ref.dtype)


def _gqa_attention(proj, batch, seq, tq=256, tk=512):
    kern = functools.partial(_gqa_kernel, tk=tk)
    return pl.pallas_call(
        kern,
        grid=(batch, seq // tq),
        in_specs=[
            pl.BlockSpec((None, tq, SELF_W), lambda b, i: (b, i, 0)),
            pl.BlockSpec((None, seq, KV_W_A), lambda b, i: (b, 0, SELF_W // KV_W_A)),
            pl.BlockSpec((None, seq, KV_W_A), lambda b, i: (b, 0, SELF_W // KV_W_A + 1)),
        ],
        out_specs=pl.BlockSpec((None, tq, SELF_W), lambda b, i: (b, i, 0)),
        out_shape=jax.ShapeDtypeStruct((batch, seq, SELF_W), BF16),
        compiler_params=_cparams(("parallel", "parallel")),
        name="gqa_attention",
    )(proj, proj, proj)


def _natten_kernel(q_ref, k_ref, v_ref, bias_ref, o_ref, *, rows, rows_per_step):
    r0 = pl.program_id(2) * rows_per_step
    win = ROW_WIN * GRID_W
    for i in range(rows_per_step):
        r = r0 + i
        sr = jnp.clip(r - ROW_WIN // 2, 0, rows - ROW_WIN)
        off = sr - r + (ROW_WIN - 1)
        start = pl.multiple_of(sr * GRID_W, GRID_W)
        k_w = k_ref[pl.ds(start, win), :]
        v_w = v_ref[pl.ds(start, win), :]
        for h in range(LANES // HEAD_DIM):
            q_h = q_ref[i * GRID_W:(i + 1) * GRID_W, h * HEAD_DIM:(h + 1) * HEAD_DIM]
            s = lax.dot_general(q_h, k_w[:, h * HEAD_DIM:(h + 1) * HEAD_DIM],
                                (((1,), (1,)), ((), ())), preferred_element_type=F32)
            s = s + bias_ref[h, off]
            m = jnp.max(s, axis=-1, keepdims=True)
            p = jnp.exp(s - m)
            l = jnp.sum(p, axis=-1, keepdims=True)
            o = jnp.dot(p.astype(BF16), v_w[:, h * HEAD_DIM:(h + 1) * HEAD_DIM],
                        preferred_element_type=F32) / l
            o_ref[i * GRID_W:(i + 1) * GRID_W, h * HEAD_DIM:(h + 1) * HEAD_DIM] = o.astype(o_ref.dtype)


def _natten_bias(rpb):
    cols = np.arange(GRID_W)
    col_start = np.clip(cols - COL_WIN // 2, 0, GRID_W - COL_WIN)
    cp = np.arange(GRID_W)
    valid = (cp[None, :] >= col_start[:, None]) & (cp[None, :] < col_start[:, None] + COL_WIN)
    dc = np.clip(cp[None, :] - cols[:, None] + (COL_WIN - 1), 0, 2 * COL_WIN - 2)
    t = rpb[:, :, dc]
    t = jnp.where(jnp.asarray(valid)[None, None], t, NEG_BIG)
    t = t.transpose(0, 2, 1, 3)
    slabs = [t[:, :, off:off + ROW_WIN, :].reshape(t.shape[0], GRID_W, ROW_WIN * GRID_W)
             for off in range(ROW_WIN)]
    return jnp.stack(slabs, axis=1).astype(F32)


def _natten(proj, bias, batch, seq, rows_per_step=8):
    rows = seq // GRID_W
    tq = rows_per_step * GRID_W
    hp = N_SELF_HEADS * HEAD_DIM // LANES
    kern = functools.partial(_natten_kernel, rows=rows, rows_per_step=rows_per_step)
    return pl.pallas_call(
        kern,
        grid=(batch, hp, rows // rows_per_step),
        in_specs=[
            pl.BlockSpec((None, tq, LANES), lambda b, h, i: (b, i, h)),
            pl.BlockSpec((None, seq, LANES), lambda b, h, i: (b, 0, hp + h)),
            pl.BlockSpec((None, seq, LANES), lambda b, h, i: (b, 0, 2 * hp + h)),
            pl.BlockSpec((LANES // HEAD_DIM, ROW_WIN, GRID_W, ROW_WIN * GRID_W), lambda b, h, i: (h, 0, 0, 0)),
        ],
        out_specs=pl.BlockSpec((None, tq, LANES), lambda b, h, i: (b, i, h)),
        out_shape=jax.ShapeDtypeStruct((batch, seq, SELF_W), BF16),
        compiler_params=_cparams(("parallel", "parallel", "arbitrary")),
        name="natten",
    )(proj, proj, proj, bias)


def _split_bf16(a):
    hi = a.astype(BF16)
    lo = (a - hi.astype(F32)).astype(BF16)
    return hi, lo


def _mix_kernel(x_ref, so_ref, qx_ref, kv_ref, wo_ref, g_ref, wrh_ref, wrl_ref,
                xo_ref, hn_ref, lg_ref):
    qx = qx_ref[...]
    heads = []
    for h in range(N_MEM_HEADS):
        sl = slice(h * HEAD_DIM, (h + 1) * HEAD_DIM)
        km = kv_ref[:, sl]
        vm = kv_ref[:, MEM_W + h * HEAD_DIM:MEM_W + (h + 1) * HEAD_DIM]
        s = lax.dot_general(qx[:, sl], km, (((1,), (1,)), ((), ())), preferred_element_type=F32)
        m = jnp.max(s, axis=-1, keepdims=True)
        p = jnp.exp(s - m)
        l = jnp.sum(p, axis=-1, keepdims=True)
        heads.append(jnp.dot(p.astype(BF16), vm, preferred_element_type=F32) / l)
    mem_out = jnp.concatenate(heads, axis=-1).astype(BF16)
    y = (x_ref[...]
         + jnp.dot(so_ref[...], wo_ref[:SELF_W, :], preferred_element_type=F32)
         + jnp.dot(mem_out, wo_ref[SELF_W:, :], preferred_element_type=F32))
    xo_ref[...] = y
    ms = jnp.mean(y * y, axis=-1, keepdims=True)
    hn = (y * lax.rsqrt(ms + EPS)) * g_ref[...]
    hn_ref[...] = hn.astype(BF16)
    hi, lo = _split_bf16(hn)
    dn = (((1,), (1,)), ((), ()))
    lg_ref[...] = (lax.dot_general(wrh_ref[...], hi, dn, preferred_element_type=F32)
                   + lax.dot_general(wrh_ref[...], lo, dn, preferred_element_type=F32)
                   + lax.dot_general(wrl_ref[...], hi, dn, preferred_element_type=F32))


def _mix(x2d, self_out2d, proj2d, qx_block, kv, wo_bf16, g_ffn, wr_t, seq, tm=256):
    n, d = x2d.shape
    blocks_per_seq = seq // tm
    wrh, wrl = _split_bf16(wr_t)
    return pl.pallas_call(
        _mix_kernel,
        grid=(n // tm,),
        in_specs=[
            pl.BlockSpec((tm, d), lambda i: (i, 0)),
            pl.BlockSpec((tm, SELF_W), lambda i: (i, 0)),
            pl.BlockSpec((tm, MEM_W), lambda i: (i, qx_block)),
            pl.BlockSpec((None, N_MEM, 2 * MEM_W), lambda i: (i // blocks_per_seq, 0, 0)),
            pl.BlockSpec((d, d), lambda i: (0, 0)),
            pl.BlockSpec((1, d), lambda i: (0, 0)),
            pl.BlockSpec((N_EXPERTS, d), lambda i: (0, 0)),
            pl.BlockSpec((N_EXPERTS, d), lambda i: (0, 0)),
        ],
        out_specs=[
            pl.BlockSpec((tm, d), lambda i: (i, 0)),
            pl.BlockSpec((tm, d), lambda i: (i, 0)),
            pl.BlockSpec((N_EXPERTS, tm), lambda i: (0, i)),
        ],
        out_shape=[
            jax.ShapeDtypeStruct((n, d), F32),
            jax.ShapeDtypeStruct((n, d), BF16),
            jax.ShapeDtypeStruct((N_EXPERTS, n), F32),
        ],
        compiler_params=_cparams(("parallel",)),
        name="mix_out_router",
    )(x2d, self_out2d, proj2d, kv, wo_bf16, g_ffn.reshape(1, d), wrh, wrl)


def _ffn_kernel(xe_ref, gate_ref, wg_ref, wu_ref, wd_ref, o_ref, wgb, wub, wdb, *, chunk):
    f = pl.program_id(1)
    n_f = pl.num_programs(1)
    wgb[...] = wg_ref[...].astype(BF16)
    wub[...] = wu_ref[...].astype(BF16)
    wdb[...] = wd_ref[...].astype(BF16)
    cap = xe_ref.shape[0]

    def body(c, _):
        rs = pl.ds(pl.multiple_of(c * chunk, chunk), chunk)
        xr = xe_ref[rs, :]
        gg = jnp.dot(xr, wgb[...], preferred_element_type=F32)
        uu = jnp.dot(xr, wub[...], preferred_element_type=F32)
        hid = (jax.nn.silu(gg) * uu).astype(BF16)
        part = jnp.dot(hid, wdb[...], preferred_element_type=F32)

        @pl.when(f == 0)
        def _():
            o_ref[rs, :] = part

        @pl.when(f > 0)
        def _():
            o_ref[rs, :] += part

        @pl.when(f == n_f - 1)
        def _():
            o_ref[rs, :] = o_ref[rs, :] * gate_ref[rs, :]
        return 0

    lax.fori_loop(0, cap // chunk, body, 0)


def _expert_ffn(xe, gates_col, w_gate, w_up, w_down, cap, tf=512, chunk=512):
    d = xe.shape[1]
    n_f = D_EXPERT // tf
    kern = functools.partial(_ffn_kernel, chunk=chunk)
    return pl.pallas_call(
        kern,
        grid=(N_EXPERTS, n_f),
        in_specs=[
            pl.BlockSpec((cap, d), lambda e, f: (e, 0)),
            pl.BlockSpec((cap, 1), lambda e, f: (e, 0)),
            pl.BlockSpec((None, d, tf), lambda e, f: (e, 0, f)),
            pl.BlockSpec((None, d, tf), lambda e, f: (e, 0, f)),
            pl.BlockSpec((None, tf, d), lambda e, f: (e, f, 0)),
        ],
        out_specs=pl.BlockSpec((cap, d), lambda e, f: (e, 0)),
        out_shape=jax.ShapeDtypeStruct((N_EXPERTS * cap, d), F32),
        scratch_shapes=[pltpu.VMEM((d, tf), BF16), pltpu.VMEM((d, tf), BF16), pltpu.VMEM((tf, d), BF16)],
        compiler_params=_cparams(("parallel", "arbitrary")),
        name="expert_ffn",
    )(xe, gates_col, w_gate, w_up, w_down)


def _tile_gain(g64, n_heads, scale=1.0):
    return jnp.tile(g64.astype(F32) * scale, n_heads)


def _trunk(x, mem, p):
    batch, seq, d = x.shape
    n = batch * seq
    cap = EC_CAPACITY_FACTOR * n // N_EXPERTS
    cos_t, sin_t = _rope_tables(seq)
    cos_m, sin_m = cos_t[:N_MEM], sin_t[:N_MEM]
    qscale = HEAD_DIM ** -0.5
    x2d = x.reshape(n, d)
    mem2d = mem.reshape(batch * N_MEM, d)
    ones_kv = jnp.ones((MEM_W,), F32)
    for i in range(DEPTH):
        j = i // 2
        xq_gain = _tile_gain(p["xq_norm"][i], N_MEM_HEADS, qscale)
        if i % 2 == 0:
            gain = jnp.concatenate([_tile_gain(p["q_norm_a"][j], N_SELF_HEADS, qscale),
                                    _tile_gain(p["k_norm_a"][j], N_KV_HEADS),
                                    jnp.ones((KV_W_A,), F32), xq_gain])
            mask = [True] * 8 + [False] * 2 + [True] * 2
            proj = _fused_proj(x2d, p["norm_mix"][i], p["w_in_a"][j].astype(BF16), gain, mask, 8,
                               cos_t, sin_t, seq)
            self_out = _gqa_attention(proj.reshape(batch, seq, IN_W_A), batch, seq)
            qx_block = (SELF_W + 2 * KV_W_A) // MEM_W
        else:
            gain = jnp.concatenate([_tile_gain(p["q_norm_b"][j], N_SELF_HEADS, qscale),
                                    _tile_gain(p["k_norm_b"][j], N_SELF_HEADS),
                                    jnp.ones((SELF_W,), F32), xq_gain])
            mask = [True] * 12 + [False] * 6 + [True] * 2
            proj = _fused_proj(x2d, p["norm_mix"][i], p["w_in_b"][j].astype(BF16), gain, mask, 0,
                               cos_t, sin_t, seq)
            self_out = _natten(proj.reshape(batch, seq, IN_W_B), _natten_bias(p["na_rpb"][j]), batch, seq)
            qx_block = 3 * SELF_W // MEM_W
        kv_gain = jnp.concatenate([_tile_gain(p["xk_norm"][i], N_MEM_HEADS), ones_kv])
        kv = _fused_proj(mem2d, p["norm_mem"][i], p["w_mem_kv"][i].astype(BF16), kv_gain,
                         [True, True, False, False], 0, cos_m, sin_m, N_MEM)
        x2d, hn, logits_t = _mix(x2d, self_out.reshape(n, SELF_W), proj, qx_block,
                                 kv.reshape(batch, N_MEM, 2 * MEM_W), p["w_o"][i].astype(BF16),
                                 p["norm_ffn"][i], p["w_router"][i].T, seq)
        aff_t = jax.nn.softmax(logits_t, axis=0)
        gates, idx = lax.top_k(aff_t, cap)
        flat_idx = idx.reshape(-1)
        xe = hn[flat_idx]
        ye = _expert_ffn(xe, gates.reshape(-1, 1), p["w_gate"][i], p["w_up"][i], p["w_down"][i], cap)
        x2d = x2d.at[flat_idx].add(ye)
    return x2d.reshape(batch, seq, d)


def kernel(x_prompt, x_sample, mem_prompt, mem_sample, norm_mix, w_in_a, q_norm_a, k_norm_a, w_in_b, q_norm_b, k_norm_b, na_rpb, norm_mem, w_mem_kv, xq_norm, xk_norm, w_o, norm_ffn, w_router, w_gate, w_up, w_down):
    p = dict(norm_mix=norm_mix, w_in_a=w_in_a, q_norm_a=q_norm_a, k_norm_a=k_norm_a, w_in_b=w_in_b,
             q_norm_b=q_norm_b, k_norm_b=k_norm_b, na_rpb=na_rpb, norm_mem=norm_mem, w_mem_kv=w_mem_kv,
             xq_norm=xq_norm, xk_norm=xk_norm, w_o=w_o, norm_ffn=norm_ffn, w_router=w_router,
             w_gate=w_gate, w_up=w_up, w_down=w_down)
    return _trunk(x_prompt, mem_prompt, p), _trunk(x_sample, mem_sample, p)
```

```python
import functools
import math

import jax
import jax.numpy as jnp
import numpy as np
from jax import lax
from jax.experimental import pallas as pl
from jax.experimental.pallas import tpu as pltpu

F32 = jnp.float32
BF16 = jnp.bfloat16

D_MODEL = 1024
DEPTH = 4
GRID_W = 64
HEAD_DIM = 64
N_SELF_HEADS = 12
N_KV_HEADS = 4
N_MEM_HEADS = 4
N_MEM = 256
SELF_W = N_SELF_HEADS * HEAD_DIM
KV_W_A = N_KV_HEADS * HEAD_DIM
MEM_W = N_MEM_HEADS * HEAD_DIM
IN_W_A = SELF_W + 2 * KV_W_A + MEM_W
IN_W_B = 3 * SELF_W + MEM_W
ROW_WIN = 8
COL_WIN = 16
ROPE_THETA = 10000.0
ROPE_AXIS_DIM = HEAD_DIM // 2
N_EXPERTS = 16
EC_CAPACITY_FACTOR = 2
D_EXPERT = 2 * D_MODEL
EPS = 1e-6

LANES = 128
VMEM_LIMIT = 56 * 1024 * 1024
NEG_BIG = -1e30


def _cparams(sem):
    return pltpu.CompilerParams(dimension_semantics=sem, vmem_limit_bytes=VMEM_LIMIT)


def _swap16(y):
    lane = lax.broadcasted_iota(jnp.int32, y.shape, 1)
    up = pltpu.roll(y, 16, axis=1)
    dn = pltpu.roll(y, LANES - 16, axis=1)
    return jnp.where((lane & 16) != 0, up, dn)


def _proj_kernel(x_ref, g_ref, w_ref, gain_ref, cos_ref, sin_ref, bd_ref, o_ref, *,
                 n_norm_slabs_mask, n_rope_slabs):
    x = x_ref[...]
    ms = jnp.mean(x * x, axis=-1, keepdims=True)
    h = (x * lax.rsqrt(ms + EPS)) * g_ref[...]
    proj = jnp.dot(h.astype(BF16), w_ref[...], preferred_element_type=F32)
    n_slabs = proj.shape[1] // LANES
    bd = bd_ref[...]
    for s in range(n_slabs):
        y = proj[:, s * LANES:(s + 1) * LANES]
        if n_norm_slabs_mask[s]:
            sq = y * y
            hi = sq.astype(BF16)
            lo = (sq - hi.astype(F32)).astype(BF16)
            ss = (jnp.dot(hi, bd, preferred_element_type=F32)
                  + jnp.dot(lo, bd, preferred_element_type=F32))
            y = (y * lax.rsqrt(ss * (1.0 / HEAD_DIM) + EPS)) * gain_ref[:, s * LANES:(s + 1) * LANES]
        if s < n_rope_slabs:
            y = y * cos_ref[...] + _swap16(y) * sin_ref[...]
        o_ref[:, s * LANES:(s + 1) * LANES] = y.astype(o_ref.dtype)


def _fused_proj(x2d, g, w_bf16, gain_full, norm_mask, n_rope_slabs, cos_t, sin_t, seq, tm=256):
    n, d = x2d.shape
    w_out = w_bf16.shape[1]
    assert n % tm == 0 and seq % tm == 0
    blocks_per_seq = seq // tm
    bd = jnp.asarray(np.kron(np.eye(LANES // HEAD_DIM), np.ones((HEAD_DIM, HEAD_DIM))), BF16)
    kern = functools.partial(_proj_kernel, n_norm_slabs_mask=tuple(norm_mask),
                             n_rope_slabs=n_rope_slabs)
    return pl.pallas_call(
        kern,
        grid=(n // tm,),
        in_specs=[
            pl.BlockSpec((tm, d), lambda i: (i, 0)),
            pl.BlockSpec((1, d), lambda i: (0, 0)),
            pl.BlockSpec((d, w_out), lambda i: (0, 0)),
            pl.BlockSpec((1, w_out), lambda i: (0, 0)),
            pl.BlockSpec((tm, LANES), lambda i: (i % blocks_per_seq, 0)),
            pl.BlockSpec((tm, LANES), lambda i: (i % blocks_per_seq, 0)),
            pl.BlockSpec((LANES, LANES), lambda i: (0, 0)),
        ],
        out_specs=pl.BlockSpec((tm, w_out), lambda i: (i, 0)),
        out_shape=jax.ShapeDtypeStruct((n, w_out), BF16),
        compiler_params=_cparams(("parallel",)),
        name="fused_proj",
    )(x2d, g.reshape(1, d), w_bf16, gain_full.reshape(1, w_out), cos_t, sin_t, bd)


def _rope_tables(seq):
    t = jnp.arange(seq)
    row = (t // GRID_W).astype(F32)
    col = (t % GRID_W).astype(F32)
    inv = 1.0 / (ROPE_THETA ** (jnp.arange(0, ROPE_AXIS_DIM, 2, dtype=F32) / ROPE_AXIS_DIM))
    ar = row[:, None] * inv[None, :]
    ac = col[:, None] * inv[None, :]
    cr, sr, cc, sc = jnp.cos(ar), jnp.sin(ar), jnp.cos(ac), jnp.sin(ac)
    cos_h = jnp.concatenate([cr, cr, cc, cc], axis=-1)
    sin_h = jnp.concatenate([-sr, sr, -sc, sc], axis=-1)
    reps = LANES // HEAD_DIM
    return jnp.tile(cos_h, (1, reps)), jnp.tile(sin_h, (1, reps))


def _gqa_kernel(q_ref, k_ref, v_ref, o_ref, *, tk):
    tq = q_ref.shape[0]
    seq = k_ref.shape[0]
    group = N_SELF_HEADS // N_KV_HEADS
    n_kv = seq // tk
    for g in range(N_KV_HEADS):
        qs = jnp.concatenate(
            [q_ref[:, (g * group + r) * HEAD_DIM:(g * group + r + 1) * HEAD_DIM] for r in range(group)],
            axis=0)
        m_rows = group * tq

        def body(j, carry, g=g, qs=qs):
            m_i, l_i, acc = carry
            start = pl.multiple_of(j * tk, tk)
            k_t = k_ref[pl.ds(start, tk), g * HEAD_DIM:(g + 1) * HEAD_DIM]
            v_t = v_ref[pl.ds(start, tk), g * HEAD_DIM:(g + 1) * HEAD_DIM]
            s = lax.dot_general(qs, k_t, (((1,), (1,)), ((), ())), preferred_element_type=F32)
            m_new = jnp.maximum(m_i, jnp.max(s, axis=-1, keepdims=True))
            alpha = jnp.exp(m_i - m_new)
            p = jnp.exp(s - m_new)
            l_new = alpha * l_i + jnp.sum(p, axis=-1, keepdims=True)
            acc_new = alpha * acc + jnp.dot(p.astype(BF16), v_t, preferred_element_type=F32)
            return m_new, l_new, acc_new

        init = (jnp.full((m_rows, 1), NEG_BIG, F32), jnp.zeros((m_rows, 1), F32),
                jnp.zeros((m_rows, HEAD_DIM), F32))
        _, l_f, acc_f = lax.fori_loop(0, n_kv, body, init)
        o = acc_f / l_f
        for r in range(group):
            hh = g * group + r
            o_ref[:, hh * HEAD_DIM:(hh + 1) * HEAD_DIM] = o[r * tq:(r + 1) * tq].astype(o_ref.dtype)


def _gqa_attention(proj, batch, seq, tq=256, tk=512):
    kern = functools.partial(_gqa_kernel, tk=tk)
    return pl.pallas_call(
        kern,
        grid=(batch, seq // tq),
        in_specs=[
            pl.BlockSpec((None, tq, SELF_W), lambda b, i: (b, i, 0)),
            pl.BlockSpec((None, seq, KV_W_A), lambda b, i: (b, 0, SELF_W // KV_W_A)),
            pl.BlockSpec((None, seq, KV_W_A), lambda b, i: (b, 0, SELF_W // KV_W_A + 1)),
        ],
        out_specs=pl.BlockSpec((None, tq, SELF_W), lambda b, i: (b, i, 0)),
        out_shape=jax.ShapeDtypeStruct((batch, seq, SELF_W), BF16),
        compiler_params=_cparams(("parallel", "parallel")),
        name="gqa_attention",
    )(proj, proj, proj)


def _natten_kernel(q_ref, k_ref, v_ref, bias_ref, o_ref, *, rows, rows_per_step):
    r0 = pl.program_id(2) * rows_per_step
    win = ROW_WIN * GRID_W
    for i in range(rows_per_step):
        r = r0 + i
        sr = jnp.clip(r - ROW_WIN // 2, 0, rows - ROW_WIN)
        off = sr - r + (ROW_WIN - 1)
        start = pl.multiple_of(sr * GRID_W, GRID_W)
        k_w = k_ref[pl.ds(start, win), :]
        v_w = v_ref[pl.ds(start, win), :]
        for h in range(LANES // HEAD_DIM):
            q_h = q_ref[i * GRID_W:(i + 1) * GRID_W, h * HEAD_DIM:(h + 1) * HEAD_DIM]
            s = lax.dot_general(q_h, k_w[:, h * HEAD_DIM:(h + 1) * HEAD_DIM],
                                (((1,), (1,)), ((), ())), preferred_element_type=F32)
            s = s + bias_ref[h, off]
            m = jnp.max(s, axis=-1, keepdims=True)
            p = jnp.exp(s - m)
            l = jnp.sum(p, axis=-1, keepdims=True)
            o = jnp.dot(p.astype(BF16), v_w[:, h * HEAD_DIM:(h + 1) * HEAD_DIM],
                        preferred_element_type=F32) / l
            o_ref[i * GRID_W:(i + 1) * GRID_W, h * HEAD_DIM:(h + 1) * HEAD_DIM] = o.astype(o_ref.dtype)


def _natten_bias(rpb):
    cols = np.arange(GRID_W)
    col_start = np.clip(cols - COL_WIN // 2, 0, GRID_W - COL_WIN)
    cp = np.arange(GRID_W)
    valid = (cp[None, :] >= col_start[:, None]) & (cp[None, :] < col_start[:, None] + COL_WIN)
    dc = np.clip(cp[None, :] - cols[:, None] + (COL_WIN - 1), 0, 2 * COL_WIN - 2)
    t = rpb[:, :, dc]
    t = jnp.where(jnp.asarray(valid)[None, None], t, NEG_BIG)
    t = t.transpose(0, 2, 1, 3)
    slabs = [t[:, :, off:off + ROW_WIN, :].reshape(t.shape[0], GRID_W, ROW_WIN * GRID_W)
             for off in range(ROW_WIN)]
    return jnp.stack(slabs, axis=1).astype(F32)


def _natten(proj, bias, batch, seq, rows_per_step=8):
    rows = seq // GRID_W
    tq = rows_per_step * GRID_W
    hp = N_SELF_HEADS * HEAD_DIM // LANES
    kern = functools.partial(_natten_kernel, rows=rows, rows_per_step=rows_per_step)
    return pl.pallas_call(
        kern,
        grid=(batch, hp, rows // rows_per_step),
        in_specs=[
            pl.BlockSpec((None, tq, LANES), lambda b, h, i: (b, i, h)),
            pl.BlockSpec((None, seq, LANES), lambda b, h, i: (b, 0, hp + h)),
            pl.BlockSpec((None, seq, LANES), lambda b, h, i: (b, 0, 2 * hp + h)),
            pl.BlockSpec((LANES // HEAD_DIM, ROW_WIN, GRID_W, ROW_WIN * GRID_W), lambda b, h, i: (h, 0, 0, 0)),
        ],
        out_specs=pl.BlockSpec((None, tq, LANES), lambda b, h, i: (b, i, h)),
        out_shape=jax.ShapeDtypeStruct((batch, seq, SELF_W), BF16),
        compiler_params=_cparams(("parallel", "parallel", "arbitrary")),
        name="natten",
    )(proj, proj, proj, bias)


def _split_bf16(a):
    hi = a.astype(BF16)
    lo = (a - hi.astype(F32)).astype(BF16)
    return hi, lo


def _mix_kernel(x_ref, so_ref, qx_ref, kv_ref, wo_ref, g_ref, wrh_ref, wrl_ref,
                xo_ref, hn_ref, lg_ref):
    qx = qx_ref[...]
    heads = []
    for h in range(N_MEM_HEADS):
        sl = slice(h * HEAD_DIM, (h + 1) * HEAD_DIM)
        km = kv_ref[:, sl]
        vm = kv_ref[:, MEM_W + h * HEAD_DIM:MEM_W + (h + 1) * HEAD_DIM]
        s = lax.dot_general(qx[:, sl], km, (((1,), (1,)), ((), ())), preferred_element_type=F32)
        m = jnp.max(s, axis=-1, keepdims=True)
        p = jnp.exp(s - m)
        l = jnp.sum(p, axis=-1, keepdims=True)
        heads.append(jnp.dot(p.astype(BF16), vm, preferred_element_type=F32) / l)
    mem_out = jnp.concatenate(heads, axis=-1).astype(BF16)
    y = (x_ref[...]
         + jnp.dot(so_ref[...], wo_ref[:SELF_W, :], preferred_element_type=F32)
         + jnp.dot(mem_out, wo_ref[SELF_W:, :], preferred_element_type=F32))
    xo_ref[...] = y
    ms = jnp.mean(y * y, axis=-1, keepdims=True)
    hn = (y * lax.rsqrt(ms + EPS)) * g_ref[...]
    bits = pltpu.bitcast(hn.astype(BF16).astype(F32), jnp.uint32)
    half = hn.shape[1] // 2
    hn_ref[...] = (bits[:, :half] >> 16) | bits[:, half:]
    hi, lo = _split_bf16(hn)
    dn = (((1,), (1,)), ((), ()))
    lg_ref[...] = (lax.dot_general(wrh_ref[...], hi, dn, preferred_element_type=F32)
                   + lax.dot_general(wrh_ref[...], lo, dn, preferred_element_type=F32)
                   + lax.dot_general(wrl_ref[...], hi, dn, preferred_element_type=F32))


def _mix(x2d, self_out2d, proj2d, qx_block, kv, wo_bf16, g_ffn, wr_t, seq, tm=256):
    n, d = x2d.shape
    blocks_per_seq = seq // tm
    wrh, wrl = _split_bf16(wr_t)
    return pl.pallas_call(
        _mix_kernel,
        grid=(n // tm,),
        in_specs=[
            pl.BlockSpec((tm, d), lambda i: (i, 0)),
            pl.BlockSpec((tm, SELF_W), lambda i: (i, 0)),
            pl.BlockSpec((tm, MEM_W), lambda i: (i, qx_block)),
            pl.BlockSpec((None, N_MEM, 2 * MEM_W), lambda i: (i // blocks_per_seq, 0, 0)),
            pl.BlockSpec((d, d), lambda i: (0, 0)),
            pl.BlockSpec((1, d), lambda i: (0, 0)),
            pl.BlockSpec((N_EXPERTS, d), lambda i: (0, 0)),
            pl.BlockSpec((N_EXPERTS, d), lambda i: (0, 0)),
        ],
        out_specs=[
            pl.BlockSpec((tm, d), lambda i: (i, 0)),
            pl.BlockSpec((tm, d // 2), lambda i: (i, 0)),
            pl.BlockSpec((N_EXPERTS, tm), lambda i: (0, i)),
        ],
        out_shape=[
            jax.ShapeDtypeStruct((n, d), F32),
            jax.ShapeDtypeStruct((n, d // 2), jnp.uint32),
            jax.ShapeDtypeStruct((N_EXPERTS, n), F32),
        ],
        compiler_params=_cparams(("parallel",)),
        name="mix_out_router",
    )(x2d, self_out2d, proj2d, kv, wo_bf16, g_ffn.reshape(1, d), wrh, wrl)


def _route_kernel(lg_ref, idx_ref, gate_ref, pos_ref, offs_ref, aff_s, gt_s, eq_s, need_s, *, cap):
    lg = lg_ref[...]
    n_e, nbk, _ = lg.shape
    ex = jnp.exp(lg - jnp.max(lg, axis=0, keepdims=True))
    aff = ex / jnp.sum(ex, axis=0, keepdims=True)
    keys = pltpu.bitcast(aff, jnp.int32)

    def count(mask_f):
        return jnp.sum(jnp.sum(mask_f, axis=1, keepdims=True), axis=2, keepdims=True)

    def bisect(it, thr):
        cand = thr | jnp.left_shift(jnp.int32(1), 30 - it)
        cnt = count(jnp.where(keys >= cand, 1.0, 0.0))
        return jnp.where(cnt >= float(cap), cand, thr)

    thr = lax.fori_loop(0, 31, bisect, jnp.zeros((n_e, 1, 1), jnp.int32))
    gt = jnp.where(keys > thr, 1.0, 0.0)
    eq = jnp.where(keys == thr, 1.0, 0.0)
    need = float(cap) - count(gt)
    aff_s[...] = aff
    gt_s[...] = gt
    eq_s[...] = eq
    need_s[...] = jnp.broadcast_to(need, need_s.shape)

    def tri(shape, fn):
        r = lax.broadcasted_iota(jnp.int32, shape, 0)
        c = lax.broadcasted_iota(jnp.int32, shape, 1)
        return jnp.where(fn(r, c), 1.0, 0.0).astype(BF16)

    u_lane = tri((LANES, LANES), lambda r, c: r <= c)
    l_blk = tri((nbk, nbk), lambda r, c: c < r)
    u_blk = tri((nbk, nbk), lambda r, c: r <= c)
    ones_l = jnp.ones((LANES, LANES), BF16)
    ones_b = jnp.ones((nbk, LANES), BF16)
    ones_8 = jnp.ones((8, LANES), BF16)
    dn_t = (((1,), (1,)), ((), ()))

    def cums(m):
        loc = jnp.dot(m.astype(BF16), u_lane, preferred_element_type=F32)
        tot = jnp.broadcast_to(loc[:, LANES - 1:LANES], (nbk, LANES))
        offs = jnp.dot(l_blk, tot.astype(BF16), preferred_element_type=F32)
        return loc, offs

    p_b = lax.broadcasted_iota(jnp.int32, (cap, nbk), 0).astype(F32)
    p_l = lax.broadcasted_iota(jnp.int32, (cap, LANES), 0).astype(F32)
    lane_b = lax.broadcasted_iota(jnp.int32, (cap, nbk), 1).astype(F32)
    lane_l = lax.broadcasted_iota(jnp.int32, (cap, LANES), 1).astype(F32)
    eye = (lax.broadcasted_iota(jnp.int32, (LANES, LANES), 0)
           == lax.broadcasted_iota(jnp.int32, (LANES, LANES), 1))

    def per_expert(e, _):
        eqm = eq_s[e]
        loc, offs = cums(eqm)
        tie_rank = loc + offs - eqm
        sel = gt_s[e] + eqm * jnp.where(tie_rank < need_s[e][0:1, 0:1], 1.0, 0.0)
        loc2, offs2 = cums(sel)
        pos_ref[e] = jnp.where(sel > 0.0, loc2 + offs2 - sel, -1.0).astype(jnp.int32)
        offs_ref[e] = offs2.astype(jnp.int32)
        tot_row = lax.dot_general(ones_8, sel.astype(BF16), dn_t, preferred_element_type=F32)
        s_row = jnp.dot(tot_row.astype(BF16), u_blk, preferred_element_type=F32)[0:1, :]
        nbv = jnp.dot(jnp.where(s_row <= p_b, 1.0, 0.0).astype(BF16), ones_b, preferred_element_type=F32)
        onehot = jnp.where(lane_b == nbv[:, :nbk], 1.0, 0.0).astype(BF16)
        g_loc = jnp.dot(onehot, loc2.astype(BF16), preferred_element_type=F32)
        o_hi = jnp.floor(offs2 * (1.0 / LANES))
        o_lo = offs2 - o_hi * LANES
        offp = (jnp.dot(onehot, o_hi.astype(BF16), preferred_element_type=F32) * LANES
                + jnp.dot(onehot, o_lo.astype(BF16), preferred_element_type=F32))
        il = jnp.dot(jnp.where(g_loc <= p_l - offp, 1.0, 0.0).astype(BF16), ones_l,
                     preferred_element_type=F32)
        tok = nbv * LANES + il
        a = aff_s[e]
        a1 = a.astype(BF16)
        r1 = a - a1.astype(F32)
        a2 = r1.astype(BF16)
        a3 = (r1 - a2.astype(F32)).astype(BF16)
        g_aff = (jnp.dot(onehot, a1, preferred_element_type=F32)
                 + jnp.dot(onehot, a2, preferred_element_type=F32)
                 + jnp.dot(onehot, a3, preferred_element_type=F32))
        gate_ref[e] = jnp.sum(jnp.where(lane_l == il, g_aff, 0.0), axis=1, keepdims=True)
        tok3 = tok.reshape(cap // LANES, LANES, LANES)
        idx_ref[e] = jnp.sum(jnp.where(eye[None], tok3, 0.0), axis=1).astype(jnp.int32)
        return 0

    lax.fori_loop(0, n_e, per_expert, 0)


def _route(logits_blk, cap):
    n_e, nbk, _ = logits_blk.shape
    return pl.pallas_call(
        functools.partial(_route_kernel, cap=cap),
        out_shape=[
            jax.ShapeDtypeStruct((n_e, cap // LANES, LANES), jnp.int32),
            jax.ShapeDtypeStruct((n_e, cap, 1), F32),
            jax.ShapeDtypeStruct((n_e, nbk, LANES), jnp.int32),
            jax.ShapeDtypeStruct((n_e, nbk, LANES), jnp.int32),
        ],
        scratch_shapes=[pltpu.VMEM((n_e, nbk, LANES), F32), pltpu.VMEM((n_e, nbk, LANES), F32),
                        pltpu.VMEM((n_e, nbk, LANES), F32), pltpu.VMEM((n_e, 8, LANES), F32)],
        compiler_params=pltpu.CompilerParams(vmem_limit_bytes=VMEM_LIMIT),
        name="route",
    )(logits_blk)


def _ffn_kernel(idx_ref, hn_hbm, gate_ref, wg_ref, wu_ref, wd_ref, o_ref, buf, sem, wgb, wub, wdb, *,
                chunk, cap, n_f):
    e = pl.program_id(0)
    f = pl.program_id(1)
    n_e = pl.num_programs(0)
    slot = e % 2
    half = wgb.shape[0] // 2

    def issue(expert, slot_, r0, count):
        def body(r, _):
            row = r0 + r
            tok = idx_ref[expert * cap + row]
            pltpu.make_async_copy(hn_hbm.at[pl.ds(tok, 1), :], buf.at[slot_, pl.ds(row, 1), :],
                                  sem.at[slot_]).start()
            return 0
        lax.fori_loop(0, count, body, 0, unroll=8)

    @pl.when((e == 0) & (f == 0))
    def _():
        issue(0, 0, 0, cap)

    @pl.when(e + 1 < n_e)
    def _():
        issue(e + 1, 1 - slot, f * (cap // n_f), cap // n_f)

    @pl.when(f == 0)
    def _():
        pltpu.make_async_copy(buf.at[slot], buf.at[slot], sem.at[slot]).wait()

    wgb[...] = wg_ref[...].astype(BF16)
    wub[...] = wu_ref[...].astype(BF16)
    wdb[...] = wd_ref[...].astype(BF16)

    def body(c, _):
        rs = pl.ds(pl.multiple_of(c * chunk, chunk), chunk)
        w = buf[slot, rs, :]
        x_lo = pltpu.bitcast(w << 16, F32).astype(BF16)
        x_hi = pltpu.bitcast(w & jnp.uint32(0xFFFF0000), F32).astype(BF16)
        gg = (jnp.dot(x_lo, wgb[:half, :], preferred_element_type=F32)
              + jnp.dot(x_hi, wgb[half:, :], preferred_element_type=F32))
        uu = (jnp.dot(x_lo, wub[:half, :], preferred_element_type=F32)
              + jnp.dot(x_hi, wub[half:, :], preferred_element_type=F32))
        hid = (jax.nn.silu(gg) * uu).astype(BF16)
        part = jnp.dot(hid, wdb[...], preferred_element_type=F32)

        @pl.when(f == 0)
        def _():
            o_ref[rs, :] = part

        @pl.when(f > 0)
        def _():
            o_ref[rs, :] += part

        @pl.when(f == n_f - 1)
        def _():
            o_ref[rs, :] = o_ref[rs, :] * gate_ref[rs, :]
        return 0

    lax.fori_loop(0, cap // chunk, body, 0)


def _expert_ffn(idx_flat, hn_packed, gates_col, w_gate, w_up, w_down, cap, tf=512, chunk=512):
    d = 2 * hn_packed.shape[1]
    n_f = D_EXPERT // tf
    assert cap % n_f == 0 and cap % chunk == 0
    kern = functools.partial(_ffn_kernel, chunk=chunk, cap=cap, n_f=n_f)
    grid_spec = pltpu.PrefetchScalarGridSpec(
        num_scalar_prefetch=1,
        grid=(N_EXPERTS, n_f),
        in_specs=[
            pl.BlockSpec(memory_space=pl.ANY),
            pl.BlockSpec((cap, 1), lambda e, f, idx: (e, 0)),
            pl.BlockSpec((None, d, tf), lambda e, f, idx: (e, 0, f)),
            pl.BlockSpec((None, d, tf), lambda e, f, idx: (e, 0, f)),
            pl.BlockSpec((None, tf, d), lambda e, f, idx: (e, f, 0)),
        ],
        out_specs=pl.BlockSpec((cap, d), lambda e, f, idx: (e, 0)),
        scratch_shapes=[pltpu.VMEM((2, cap, d // 2), jnp.uint32), pltpu.SemaphoreType.DMA((2,)),
                        pltpu.VMEM((d, tf), BF16), pltpu.VMEM((d, tf), BF16), pltpu.VMEM((tf, d), BF16)],
    )
    return pl.pallas_call(
        kern,
        grid_spec=grid_spec,
        out_shape=jax.ShapeDtypeStruct((N_EXPERTS * cap, d), F32),
        compiler_params=_cparams(("arbitrary", "arbitrary")),
        name="expert_ffn",
    )(idx_flat, hn_packed, gates_col, w_gate, w_up, w_down)


SEG = 64
SEG_AL = SEG + 8


def _combine_kernel(offs_ref, y_ref, pos_ref, jl_ref, rmat_ref, ye_hbm, o_ref, stag, sem, *,
                    cap, nbk, blocks_per_tile):
    i = pl.program_id(0)
    n_t = pl.num_programs(0)
    slot = i % 2
    tile_rows = y_ref.shape[0]

    def seg(tile, e, k):
        base = offs_ref[e * (nbk + 1) + tile * blocks_per_tile] + k * SEG
        src = jnp.minimum((base // 8) * 8, cap - SEG_AL)
        return base, src

    def issue(tile, k, slot_):
        for e in range(N_EXPERTS):
            _, src = seg(tile, e, k)
            pltpu.make_async_copy(ye_hbm.at[pl.ds(pl.multiple_of(e * cap + src, 8), SEG_AL), :],
                                  stag.at[slot_, pl.ds(e * SEG_AL, SEG_AL), :], sem.at[slot_]).start()

    def wait(slot_):
        pltpu.make_async_copy(stag.at[slot_], stag.at[slot_], sem.at[slot_]).wait()

    def contrib(k):
        pos = pos_ref[...]
        lane_e = lax.broadcasted_iota(jnp.int32, pos.shape, 1)
        base_v = jnp.zeros(pos.shape, jnp.int32)
        src_v = jnp.zeros(pos.shape, jnp.int32)
        for e in range(N_EXPERTS):
            base, src = seg(i, e, k)
            base_v = jnp.where(lane_e == e, base, base_v)
            src_v = jnp.where(lane_e == e, src, src_v)
        valid = (pos >= base_v) & (pos < base_v + SEG)
        rel = jnp.where(valid, pos - src_v, -1).astype(F32).astype(BF16)
        rep = jnp.dot(rel, rmat_ref[...], preferred_element_type=F32)
        onehot = jnp.where(rep == jl_ref[...], 1.0, 0.0).astype(BF16)
        st = stag[slot]
        hi = st.astype(BF16)
        lo = (st - hi.astype(F32)).astype(BF16)
        return (jnp.dot(onehot, hi, preferred_element_type=F32)
                + jnp.dot(onehot, lo, preferred_element_type=F32))

    @pl.when(i == 0)
    def _():
        issue(0, 0, 0)

    @pl.when(i + 1 < n_t)
    def _():
        issue(i + 1, 0, 1 - slot)

    wait(slot)
    acc = y_ref[...] + contrib(0)

    max_cnt = jnp.int32(0)
    for e in range(N_EXPERTS):
        o0 = offs_ref[e * (nbk + 1) + i * blocks_per_tile]
        o1 = offs_ref[e * (nbk + 1) + (i + 1) * blocks_per_tile]
        max_cnt = jnp.maximum(max_cnt, o1 - o0)
    n_rounds = (max_cnt + SEG - 1) // SEG

    def extra(k, acc_):
        issue(i, k, slot)
        wait(slot)
        return acc_ + contrib(k)

    o_ref[...] = lax.fori_loop(1, n_rounds, extra, acc)


def _combine(offs_flat, y2d, pos_t, ye, cap, nbk, tile=256):
    n, d = y2d.shape
    ktot = N_EXPERTS * SEG_AL
    lane = np.arange(ktot)
    jl = jnp.asarray((lane % SEG_AL)[None, :], F32)
    rmat = jnp.asarray((lane[None, :] // SEG_AL) == np.arange(N_EXPERTS)[:, None], BF16)
    kern = functools.partial(_combine_kernel, cap=cap, nbk=nbk, blocks_per_tile=tile // LANES)
    grid_spec = pltpu.PrefetchScalarGridSpec(
        num_scalar_prefetch=1,
        grid=(n // tile,),
        in_specs=[
            pl.BlockSpec((tile, d), lambda i, offs: (i, 0)),
            pl.BlockSpec((tile, N_EXPERTS), lambda i, offs: (i, 0)),
            pl.BlockSpec((1, ktot), lambda i, offs: (0, 0)),
            pl.BlockSpec((N_EXPERTS, ktot), lambda i, offs: (0, 0)),
            pl.BlockSpec(memory_space=pl.ANY),
        ],
        out_specs=pl.BlockSpec((tile, d), lambda i, offs: (i, 0)),
        scratch_shapes=[pltpu.VMEM((2, ktot, d), F32), pltpu.SemaphoreType.DMA((2,))],
    )
    return pl.pallas_call(
        kern,
        grid_spec=grid_spec,
        out_shape=jax.ShapeDtypeStruct((n, d), F32),
        compiler_params=_cparams(("arbitrary",)),
        name="combine",
    )(offs_flat, y2d, pos_t, jl, rmat, ye)


def _tile_gain(g64, n_heads, scale=1.0):
    return jnp.tile(g64.astype(F32) * scale, n_heads)


def _trunk(x, mem, p):
    batch, seq, d = x.shape
    n = batch * seq
    cap = EC_CAPACITY_FACTOR * n // N_EXPERTS
    cos_t, sin_t = _rope_tables(seq)
    cos_m, sin_m = cos_t[:N_MEM], sin_t[:N_MEM]
    qscale = HEAD_DIM ** -0.5
    x2d = x.reshape(n, d)
    mem2d = mem.reshape(batch * N_MEM, d)
    ones_kv = jnp.ones((MEM_W,), F32)
    for i in range(DEPTH):
        j = i // 2
        xq_gain = _tile_gain(p["xq_norm"][i], N_MEM_HEADS, qscale)
        if i % 2 == 0:
            gain = jnp.concatenate([_tile_gain(p["q_norm_a"][j], N_SELF_HEADS, qscale),
                                    _tile_gain(p["k_norm_a"][j], N_KV_HEADS),
                                    jnp.ones((KV_W_A,), F32), xq_gain])
            mask = [True] * 8 + [False] * 2 + [True] * 2
            proj = _fused_proj(x2d, p["norm_mix"][i], p["w_in_a"][j].astype(BF16), gain, mask, 8,
                               cos_t, sin_t, seq)
            self_out = _gqa_attention(proj.reshape(batch, seq, IN_W_A), batch, seq)
            qx_block = (SELF_W + 2 * KV_W_A) // MEM_W
        else:
            gain = jnp.concatenate([_tile_gain(p["q_norm_b"][j], N_SELF_HEADS, qscale),
                                    _tile_gain(p["k_norm_b"][j], N_SELF_HEADS),
                                    jnp.ones((SELF_W,), F32), xq_gain])
            mask = [True] * 12 + [False] * 6 + [True] * 2
            proj = _fused_proj(x2d, p["norm_mix"][i], p["w_in_b"][j].astype(BF16), gain, mask, 0,
                               cos_t, sin_t, seq)
            self_out = _natten(proj.reshape(batch, seq, IN_W_B), _natten_bias(p["na_rpb"][j]), batch, seq)
            qx_block = 3 * SELF_W // MEM_W
        kv_gain = jnp.concatenate([_tile_gain(p["xk_norm"][i], N_MEM_HEADS), ones_kv])
        kv = _fused_proj(mem2d, p["norm_mem"][i], p["w_mem_kv"][i].astype(BF16), kv_gain,
                         [True, True, False, False], 0, cos_m, sin_m, N_MEM)
        x2d, hn, logits_t = _mix(x2d, self_out.reshape(n, SELF_W), proj, qx_block,
                                 kv.reshape(batch, N_MEM, 2 * MEM_W), p["w_o"][i].astype(BF16),
                                 p["norm_ffn"][i], p["w_router"][i].T, seq)
        nbk = n // LANES
        idx, gates, pos, offs = _route(logits_t.reshape(N_EXPERTS, nbk, LANES), cap)
        ye = _expert_ffn(idx.reshape(-1), hn, gates.reshape(-1, 1),
                         p["w_gate"][i], p["w_up"][i], p["w_down"][i], cap)
        offs_flat = jnp.concatenate([offs[:, :, 0], jnp.full((N_EXPERTS, 1), cap, jnp.int32)],
                                    axis=1).reshape(-1)
        x2d = _combine(offs_flat, x2d, pos.reshape(N_EXPERTS, n).T, ye, cap, nbk)
    return x2d.reshape(batch, seq, d)


def kernel(x_prompt, x_sample, mem_prompt, mem_sample, norm_mix, w_in_a, q_norm_a, k_norm_a, w_in_b, q_norm_b, k_norm_b, na_rpb, norm_mem, w_mem_kv, xq_norm, xk_norm, w_o, norm_ffn, w_router, w_gate, w_up, w_down):
    p = dict(norm_mix=norm_mix, w_in_a=w_in_a, q_norm_a=q_norm_a, k_norm_a=k_norm_a, w_in_b=w_in_b,
             q_norm_b=q_norm_b, k_norm_b=k_norm_b, na_rpb=na_rpb, norm_mem=norm_mem, w_mem_kv=w_mem_kv,
             xq_norm=xq_norm, xk_norm=xk_norm, w_o=w_o, norm_ffn=norm_ffn, w_router=w_router,
             w_gate=w_gate, w_up=w_up, w_down=w_down)
    return _trunk(x_prompt, mem_prompt, p), _trunk(x_sample, mem_sample, p)
```

```python
import functools
import math

import jax
import jax.numpy as jnp
import numpy as np
from jax import lax
from jax.experimental import pallas as pl
from jax.experimental.pallas import tpu as pltpu

F32 = jnp.float32
BF16 = jnp.bfloat16

D_MODEL = 1024
DEPTH = 4
GRID_W = 64
HEAD_DIM = 64
N_SELF_HEADS = 12
N_KV_HEADS = 4
N_MEM_HEADS = 4
N_MEM = 256
SELF_W = N_SELF_HEADS * HEAD_DIM
KV_W_A = N_KV_HEADS * HEAD_DIM
MEM_W = N_MEM_HEADS * HEAD_DIM
IN_W_A = SELF_W + 2 * KV_W_A + MEM_W
IN_W_B = 3 * SELF_W + MEM_W
ROW_WIN = 8
COL_WIN = 16
ROPE_THETA = 10000.0
ROPE_AXIS_DIM = HEAD_DIM // 2
N_EXPERTS = 16
EC_CAPACITY_FACTOR = 2
D_EXPERT = 2 * D_MODEL
EPS = 1e-6

LANES = 128
VMEM_LIMIT = 56 * 1024 * 1024
NEG_BIG = -1e30
LOG2E = math.log2(math.e)


def _cparams(sem):
    return pltpu.CompilerParams(dimension_semantics=sem, vmem_limit_bytes=VMEM_LIMIT)


def _swap16(y):
    lane = lax.broadcasted_iota(jnp.int32, y.shape, 1)
    up = pltpu.roll(y, 16, axis=1)
    dn = pltpu.roll(y, LANES - 16, axis=1)
    return jnp.where((lane & 16) != 0, up, dn)


def _proj_kernel(x_ref, g_ref, w_ref, gain_ref, cos_ref, sin_ref, bd_ref, o_ref, *,
                 n_norm_slabs_mask, n_rope_slabs):
    x = x_ref[...]
    ms = jnp.mean(x * x, axis=-1, keepdims=True)
    h = (x * lax.rsqrt(ms + EPS)) * g_ref[...]
    proj = jnp.dot(h.astype(BF16), w_ref[...], preferred_element_type=F32)
    n_slabs = proj.shape[1] // LANES
    bd = bd_ref[...]
    for s in range(n_slabs):
        y = proj[:, s * LANES:(s + 1) * LANES]
        if n_norm_slabs_mask[s]:
            sq = y * y
            hi = sq.astype(BF16)
            lo = (sq - hi.astype(F32)).astype(BF16)
            ss = (jnp.dot(hi, bd, preferred_element_type=F32)
                  + jnp.dot(lo, bd, preferred_element_type=F32))
            y = (y * lax.rsqrt(ss * (1.0 / HEAD_DIM) + EPS)) * gain_ref[:, s * LANES:(s + 1) * LANES]
        if s < n_rope_slabs:
            y = y * cos_ref[...] + _swap16(y) * sin_ref[...]
        o_ref[:, s * LANES:(s + 1) * LANES] = y.astype(o_ref.dtype)


def _fused_proj(x2d, g, w_bf16, gain_full, norm_mask, n_rope_slabs, cos_t, sin_t, seq, tm=256):
    n, d = x2d.shape
    w_out = w_bf16.shape[1]
    assert n % tm == 0 and seq % tm == 0
    blocks_per_seq = seq // tm
    bd = jnp.asarray(np.kron(np.eye(LANES // HEAD_DIM), np.ones((HEAD_DIM, HEAD_DIM))), BF16)
    kern = functools.partial(_proj_kernel, n_norm_slabs_mask=tuple(norm_mask),
                             n_rope_slabs=n_rope_slabs)
    return pl.pallas_call(
        kern,
        grid=(n // tm,),
        in_specs=[
            pl.BlockSpec((tm, d), lambda i: (i, 0)),
            pl.BlockSpec((1, d), lambda i: (0, 0)),
            pl.BlockSpec((d, w_out), lambda i: (0, 0)),
            pl.BlockSpec((1, w_out), lambda i: (0, 0)),
            pl.BlockSpec((tm, LANES), lambda i: (i % blocks_per_seq, 0)),
            pl.BlockSpec((tm, LANES), lambda i: (i % blocks_per_seq, 0)),
            pl.BlockSpec((LANES, LANES), lambda i: (0, 0)),
        ],
        out_specs=pl.BlockSpec((tm, w_out), lambda i: (i, 0)),
        out_shape=jax.ShapeDtypeStruct((n, w_out), BF16),
        compiler_params=_cparams(("parallel",)),
        name="fused_proj",
    )(x2d, g.reshape(1, d), w_bf16, gain_full.reshape(1, w_out), cos_t, sin_t, bd)


def _rope_tables(seq):
    t = jnp.arange(seq)
    row = (t // GRID_W).astype(F32)
    col = (t % GRID_W).astype(F32)
    inv = 1.0 / (ROPE_THETA ** (jnp.arange(0, ROPE_AXIS_DIM, 2, dtype=F32) / ROPE_AXIS_DIM))
    ar = row[:, None] * inv[None, :]
    ac = col[:, None] * inv[None, :]
    cr, sr, cc, sc = jnp.cos(ar), jnp.sin(ar), jnp.cos(ac), jnp.sin(ac)
    cos_h = jnp.concatenate([cr, cr, cc, cc], axis=-1)
    sin_h = jnp.concatenate([-sr, sr, -sc, sc], axis=-1)
    reps = LANES // HEAD_DIM
    return jnp.tile(cos_h, (1, reps)), jnp.tile(sin_h, (1, reps))


def _gqa_kernel(q_ref, k_ref, v_ref, o_ref, qs_ref, vx_ref, s_ref, p_ref, *, tk, rb):
    tq = q_ref.shape[0]
    seq = k_ref.shape[0]
    group = N_SELF_HEADS // N_KV_HEADS
    n_kv = seq // tk
    for g in range(N_KV_HEADS):
        for r in range(group):
            hh = g * group + r
            qs_ref[g, r * tq:(r + 1) * tq, :] = q_ref[:, hh * HEAD_DIM:(hh + 1) * HEAD_DIM]
    m_rows = group * tq

    @pl.when(pl.program_id(1) == 0)
    def _():
        lane = lax.broadcasted_iota(jnp.int32, (seq, LANES - HEAD_DIM), 1)
        ones_col = jnp.where(lane == 0, 1.0, 0.0).astype(BF16)
        for g in range(N_KV_HEADS):
            vx_ref[:, g * LANES:g * LANES + HEAD_DIM] = v_ref[:, g * HEAD_DIM:(g + 1) * HEAD_DIM]
            vx_ref[:, g * LANES + HEAD_DIM:(g + 1) * LANES] = ones_col

    def body(j, carry):
        start = pl.multiple_of(j * tk, tk)
        out = []
        for g in range(N_KV_HEADS):
            m_prev, acc_prev = carry[g]
            sb = g % s_ref.shape[0]
            k_t = k_ref[pl.ds(start, tk), g * HEAD_DIM:(g + 1) * HEAD_DIM]
            s_ref[sb] = lax.dot_general(qs_ref[g], k_t, (((1,), (1,)), ((), ())),
                                        preferred_element_type=F32)
            m_new = jnp.maximum(m_prev, jnp.max(s_ref[sb], axis=-1, keepdims=True))
            alpha = jnp.exp2(m_prev - m_new)
            for rc in range(m_rows // rb):
                rows = slice(rc * rb, (rc + 1) * rb)
                p_ref[sb, rows, :] = jnp.exp2(s_ref[sb, rows, :] - m_new[rows]).astype(BF16)
            pv = jnp.dot(p_ref[sb], vx_ref[pl.ds(start, tk), g * LANES:(g + 1) * LANES],
                         preferred_element_type=F32)
            out.append((m_new, alpha * acc_prev + pv))
        return tuple(out)

    init = tuple((jnp.full((m_rows, 1), NEG_BIG, F32), jnp.zeros((m_rows, LANES), F32))
                 for _ in range(N_KV_HEADS))
    final = lax.fori_loop(0, n_kv, body, init)
    for g in range(N_KV_HEADS):
        acc = final[g][1]
        o = acc[:, :HEAD_DIM] / acc[:, HEAD_DIM:HEAD_DIM + 1]
        for r in range(group):
            hh = g * group + r
            o_ref[:, hh * HEAD_DIM:(hh + 1) * HEAD_DIM] = o[r * tq:(r + 1) * tq].astype(o_ref.dtype)


def _gqa_attention(proj, batch, seq, tq=256, tk=1024, rb=32):
    kern = functools.partial(_gqa_kernel, tk=tk, rb=rb)
    m_rows = (N_SELF_HEADS // N_KV_HEADS) * tq
    return pl.pallas_call(
        kern,
        grid=(batch, seq // tq),
        in_specs=[
            pl.BlockSpec((None, tq, SELF_W), lambda b, i: (b, i, 0)),
            pl.BlockSpec((None, seq, KV_W_A), lambda b, i: (b, 0, SELF_W // KV_W_A)),
            pl.BlockSpec((None, seq, KV_W_A), lambda b, i: (b, 0, SELF_W // KV_W_A + 1)),
        ],
        out_specs=pl.BlockSpec((None, tq, SELF_W), lambda b, i: (b, i, 0)),
        out_shape=jax.ShapeDtypeStruct((batch, seq, SELF_W), BF16),
        scratch_shapes=[pltpu.VMEM((N_KV_HEADS, m_rows, HEAD_DIM), BF16),
                        pltpu.VMEM((seq, N_KV_HEADS * LANES), BF16),
                        pltpu.VMEM((2, m_rows, tk), F32),
                        pltpu.VMEM((2, m_rows, tk), BF16)],
        compiler_params=_cparams(("parallel", "arbitrary")),
        name="gqa_attention",
    )(proj, proj, proj)


NA_Q_ROWS = 8
NA_K_ROWS = NA_Q_ROWS + ROW_WIN


def _natten_kernel(q_ref, k_ref, v_ref, bias_ref, o_ref, s_ref, p_ref, *, rows, rb):
    i = pl.program_id(2)
    n_i = pl.num_programs(2)
    ws = jnp.clip(i * NA_Q_ROWS - ROW_WIN // 2, 0, rows - NA_K_ROWS)
    start = pl.multiple_of(ws * GRID_W, GRID_W)
    variant = jnp.where(i == 0, 0, jnp.where(i == n_i - 1, 2, 1))
    nq = NA_Q_ROWS * GRID_W
    nk = NA_K_ROWS * GRID_W
    lane = lax.broadcasted_iota(jnp.int32, (nk, LANES - HEAD_DIM), 1)
    ones_col = jnp.where(lane == 0, 1.0, 0.0).astype(BF16)
    for h in range(LANES // HEAD_DIM):
        hs = slice(h * HEAD_DIM, (h + 1) * HEAD_DIM)
        k_w = k_ref[pl.ds(start, nk), hs]
        vx = jnp.concatenate([v_ref[pl.ds(start, nk), hs], ones_col], axis=1)
        s_ref[h] = (lax.dot_general(q_ref[:, hs], k_w, (((1,), (1,)), ((), ())),
                                    preferred_element_type=F32) + bias_ref[h, variant])
        m = jnp.max(s_ref[h], axis=-1, keepdims=True)
        for rc in range(nq // rb):
            rs = slice(rc * rb, (rc + 1) * rb)
            p_ref[h, rs, :] = jnp.exp2(s_ref[h, rs, :] - m[rs]).astype(BF16)
        acc = jnp.dot(p_ref[h], vx, preferred_element_type=F32)
        o_ref[:, hs] = (acc[:, :HEAD_DIM] / acc[:, HEAD_DIM:HEAD_DIM + 1]).astype(o_ref.dtype)


def _natten_bias(rpb):
    cols = np.arange(GRID_W)
    col_start = np.clip(cols - COL_WIN // 2, 0, GRID_W - COL_WIN)
    cp = np.arange(GRID_W)
    col_ok = (cp[None, :] >= col_start[:, None]) & (cp[None, :] < col_start[:, None] + COL_WIN)
    dc = np.clip(cp[None, :] - cols[:, None] + (COL_WIN - 1), 0, 2 * COL_WIN - 2)
    t = rpb[:, :, dc] * LOG2E
    qi = np.arange(NA_Q_ROWS)[None, :, None]
    kw = np.arange(NA_K_ROWS)[None, None, :]
    delta = np.array([0, ROW_WIN // 2, ROW_WIN])[:, None, None]
    sr = np.clip(qi + delta - ROW_WIN // 2, 0, NA_K_ROWS - ROW_WIN)
    row_ok = (kw >= sr) & (kw < sr + ROW_WIN)
    dr = np.clip(kw - qi - delta + (ROW_WIN - 1), 0, 2 * ROW_WIN - 2)
    b = t[:, dr]
    ok = row_ok[:, :, :, None, None] & col_ok[None, None, None]
    b = jnp.where(jnp.asarray(ok)[None], b, NEG_BIG)
    b = b.transpose(0, 1, 2, 4, 3, 5)
    return b.reshape(b.shape[0], 3, NA_Q_ROWS * GRID_W, NA_K_ROWS * GRID_W).astype(F32)


def _natten(proj, bias, batch, seq, rb=32):
    rows = seq // GRID_W
    nq = NA_Q_ROWS * GRID_W
    nk = NA_K_ROWS * GRID_W
    hp = N_SELF_HEADS * HEAD_DIM // LANES
    hpb = LANES // HEAD_DIM
    assert rows % NA_Q_ROWS == 0 and rows // NA_Q_ROWS >= 2 and rows >= NA_K_ROWS
    kern = functools.partial(_natten_kernel, rows=rows, rb=rb)
    return pl.pallas_call(
        kern,
        grid=(hp, batch, rows // NA_Q_ROWS),
        in_specs=[
            pl.BlockSpec((None, nq, LANES), lambda h, b, i: (b, i, h)),
            pl.BlockSpec((None, seq, LANES), lambda h, b, i: (b, 0, hp + h)),
            pl.BlockSpec((None, seq, LANES), lambda h, b, i: (b, 0, 2 * hp + h)),
            pl.BlockSpec((hpb, 3, nq, nk), lambda h, b, i: (h, 0, 0, 0)),
        ],
        out_specs=pl.BlockSpec((None, nq, LANES), lambda h, b, i: (b, i, h)),
        out_shape=jax.ShapeDtypeStruct((batch, seq, SELF_W), BF16),
        scratch_shapes=[pltpu.VMEM((hpb, nq, nk), F32), pltpu.VMEM((hpb, nq, nk), BF16)],
        compiler_params=_cparams(("parallel", "parallel", "arbitrary")),
        name="natten",
    )(proj, proj, proj, bias)


def _split_bf16(a):
    hi = a.astype(BF16)
    lo = (a - hi.astype(F32)).astype(BF16)
    return hi, lo


def _mix_kernel(x_ref, so_ref, qx_ref, kv_ref, wo_ref, g_ref, wrh_ref, wrl_ref,
                xo_ref, hn_ref, lg_ref):
    qx = qx_ref[...]
    heads = []
    for h in range(N_MEM_HEADS):
        sl = slice(h * HEAD_DIM, (h + 1) * HEAD_DIM)
        km = kv_ref[:, sl]
        vm = kv_ref[:, MEM_W + h * HEAD_DIM:MEM_W + (h + 1) * HEAD_DIM]
        s = lax.dot_general(qx[:, sl], km, (((1,), (1,)), ((), ())), preferred_element_type=F32)
        m = jnp.max(s, axis=-1, keepdims=True)
        p = jnp.exp2(s - m)
        l = jnp.sum(p, axis=-1, keepdims=True)
        heads.append(jnp.dot(p.astype(BF16), vm, preferred_element_type=F32) / l)
    mem_out = jnp.concatenate(heads, axis=-1).astype(BF16)
    y = (x_ref[...]
         + jnp.dot(so_ref[...], wo_ref[:SELF_W, :], preferred_element_type=F32)
         + jnp.dot(mem_out, wo_ref[SELF_W:, :], preferred_element_type=F32))
    xo_ref[...] = y
    ms = jnp.mean(y * y, axis=-1, keepdims=True)
    hn = (y * lax.rsqrt(ms + EPS)) * g_ref[...]
    bits = pltpu.bitcast(hn.astype(BF16).astype(F32), jnp.uint32)
    half = hn.shape[1] // 2
    hn_ref[...] = (bits[:, :half] >> 16) | bits[:, half:]
    hi, lo = _split_bf16(hn)
    dn = (((1,), (1,)), ((), ()))
    lg_ref[...] = (lax.dot_general(wrh_ref[...], hi, dn, preferred_element_type=F32)
                   + lax.dot_general(wrh_ref[...], lo, dn, preferred_element_type=F32)
                   + lax.dot_general(wrl_ref[...], hi, dn, preferred_element_type=F32))


def _mix(x2d, self_out2d, proj2d, qx_block, kv, wo_bf16, g_ffn, wr_t, seq, tm=256):
    n, d = x2d.shape
    blocks_per_seq = seq // tm
    wrh, wrl = _split_bf16(wr_t)
    return pl.pallas_call(
        _mix_kernel,
        grid=(n // tm,),
        in_specs=[
            pl.BlockSpec((tm, d), lambda i: (i, 0)),
            pl.BlockSpec((tm, SELF_W), lambda i: (i, 0)),
            pl.BlockSpec((tm, MEM_W), lambda i: (i, qx_block)),
            pl.BlockSpec((None, N_MEM, 2 * MEM_W), lambda i: (i // blocks_per_seq, 0, 0)),
            pl.BlockSpec((d, d), lambda i: (0, 0)),
            pl.BlockSpec((1, d), lambda i: (0, 0)),
            pl.BlockSpec((N_EXPERTS, d), lambda i: (0, 0)),
            pl.BlockSpec((N_EXPERTS, d), lambda i: (0, 0)),
        ],
        out_specs=[
            pl.BlockSpec((tm, d), lambda i: (i, 0)),
            pl.BlockSpec((tm, d // 2), lambda i: (i, 0)),
            pl.BlockSpec((N_EXPERTS, tm), lambda i: (0, i)),
        ],
        out_shape=[
            jax.ShapeDtypeStruct((n, d), F32),
            jax.ShapeDtypeStruct((n, d // 2), jnp.uint32),
            jax.ShapeDtypeStruct((N_EXPERTS, n), F32),
        ],
        compiler_params=_cparams(("parallel",)),
        name="mix_out_router",
    )(x2d, self_out2d, proj2d, kv, wo_bf16, g_ffn.reshape(1, d), wrh, wrl)


def _route_kernel(lg_ref, idx_ref, gate_ref, pos_ref, offs_ref, aff_s, gt_s, eq_s, need_s, *, cap):
    lg = lg_ref[...]
    n_e, nbk, _ = lg.shape
    ex = jnp.exp(lg - jnp.max(lg, axis=0, keepdims=True))
    aff = ex / jnp.sum(ex, axis=0, keepdims=True)
    keys = pltpu.bitcast(aff, jnp.int32)

    def count(mask_f):
        return jnp.sum(jnp.sum(mask_f, axis=1, keepdims=True), axis=2, keepdims=True)

    def bisect(it, thr):
        cand = thr | jnp.left_shift(jnp.int32(1), 30 - it)
        cnt = count(jnp.where(keys >= cand, 1.0, 0.0))
        return jnp.where(cnt >= float(cap), cand, thr)

    thr = lax.fori_loop(0, 31, bisect, jnp.zeros((n_e, 1, 1), jnp.int32))
    gt = jnp.where(keys > thr, 1.0, 0.0)
    eq = jnp.where(keys == thr, 1.0, 0.0)
    need = float(cap) - count(gt)
    aff_s[...] = aff
    gt_s[...] = gt
    eq_s[...] = eq
    need_s[...] = jnp.broadcast_to(need, need_s.shape)

    def tri(shape, fn):
        r = lax.broadcasted_iota(jnp.int32, shape, 0)
        c = lax.broadcasted_iota(jnp.int32, shape, 1)
        return jnp.where(fn(r, c), 1.0, 0.0).astype(BF16)

    u_lane = tri((LANES, LANES), lambda r, c: r <= c)
    l_blk = tri((nbk, nbk), lambda r, c: c < r)
    u_blk = tri((nbk, nbk), lambda r, c: r <= c)
    ones_l = jnp.ones((LANES, LANES), BF16)
    ones_b = jnp.ones((nbk, LANES), BF16)
    ones_8 = jnp.ones((8, LANES), BF16)
    dn_t = (((1,), (1,)), ((), ()))

    def cums(m):
        loc = jnp.dot(m.astype(BF16), u_lane, preferred_element_type=F32)
        tot = jnp.broadcast_to(loc[:, LANES - 1:LANES], (nbk, LANES))
        offs = jnp.dot(l_blk, tot.astype(BF16), preferred_element_type=F32)
        return loc, offs

    p_b = lax.broadcasted_iota(jnp.int32, (cap, nbk), 0).astype(F32)
    p_l = lax.broadcasted_iota(jnp.int32, (cap, LANES), 0).astype(F32)
    lane_b = lax.broadcasted_iota(jnp.int32, (cap, nbk), 1).astype(F32)
    lane_l = lax.broadcasted_iota(jnp.int32, (cap, LANES), 1).astype(F32)
    eye = (lax.broadcasted_iota(jnp.int32, (LANES, LANES), 0)
           == lax.broadcasted_iota(jnp.int32, (LANES, LANES), 1))

    def per_expert(e, _):
        eqm = eq_s[e]
        loc, offs = cums(eqm)
        tie_rank = loc + offs - eqm
        sel = gt_s[e] + eqm * jnp.where(tie_rank < need_s[e][0:1, 0:1], 1.0, 0.0)
        loc2, offs2 = cums(sel)
        pos_ref[e] = jnp.where(sel > 0.0, loc2 + offs2 - sel, -1.0).astype(jnp.int32)
        offs_ref[e] = offs2.astype(jnp.int32)
        tot_row = lax.dot_general(ones_8, sel.astype(BF16), dn_t, preferred_element_type=F32)
        s_row = jnp.dot(tot_row.astype(BF16), u_blk, preferred_element_type=F32)[0:1, :]
        nbv = jnp.dot(jnp.where(s_row <= p_b, 1.0, 0.0).astype(BF16), ones_b, preferred_element_type=F32)
        onehot = jnp.where(lane_b == nbv[:, :nbk], 1.0, 0.0).astype(BF16)
        g_loc = jnp.dot(onehot, loc2.astype(BF16), preferred_element_type=F32)
        o_hi = jnp.floor(offs2 * (1.0 / LANES))
        o_lo = offs2 - o_hi * LANES
        offp = (jnp.dot(onehot, o_hi.astype(BF16), preferred_element_type=F32) * LANES
                + jnp.dot(onehot, o_lo.astype(BF16), preferred_element_type=F32))
        il = jnp.dot(jnp.where(g_loc <= p_l - offp, 1.0, 0.0).astype(BF16), ones_l,
                     preferred_element_type=F32)
        tok = nbv * LANES + il
        a = aff_s[e]
        a1 = a.astype(BF16)
        r1 = a - a1.astype(F32)
        a2 = r1.astype(BF16)
        a3 = (r1 - a2.astype(F32)).astype(BF16)
        g_aff = (jnp.dot(onehot, a1, preferred_element_type=F32)
                 + jnp.dot(onehot, a2, preferred_element_type=F32)
                 + jnp.dot(onehot, a3, preferred_element_type=F32))
        gate_ref[e] = jnp.sum(jnp.where(lane_l == il, g_aff, 0.0), axis=1, keepdims=True)
        tok3 = tok.reshape(cap // LANES, LANES, LANES)
        idx_ref[e] = jnp.sum(jnp.where(eye[None], tok3, 0.0), axis=1).astype(jnp.int32)
        return 0

    lax.fori_loop(0, n_e, per_expert, 0)


def _route(logits_blk, cap):
    n_e, nbk, _ = logits_blk.shape
    return pl.pallas_call(
        functools.partial(_route_kernel, cap=cap),
        out_shape=[
            jax.ShapeDtypeStruct((n_e, cap // LANES, LANES), jnp.int32),
            jax.ShapeDtypeStruct((n_e, cap, 1), F32),
            jax.ShapeDtypeStruct((n_e, nbk, LANES), jnp.int32),
            jax.ShapeDtypeStruct((n_e, nbk, LANES), jnp.int32),
        ],
        scratch_shapes=[pltpu.VMEM((n_e, nbk, LANES), F32), pltpu.VMEM((n_e, nbk, LANES), F32),
                        pltpu.VMEM((n_e, nbk, LANES), F32), pltpu.VMEM((n_e, 8, LANES), F32)],
        compiler_params=pltpu.CompilerParams(vmem_limit_bytes=VMEM_LIMIT),
        name="route",
    )(logits_blk)


def _ffn_kernel(idx_ref, hn_hbm, gate_ref, wg_ref, wu_ref, wd_ref, o_ref, buf, sem, wgb, wub, wdb, *,
                chunk, cap, n_f):
    e = pl.program_id(0)
    f = pl.program_id(1)
    n_e = pl.num_programs(0)
    slot = e % 2
    half = wgb.shape[0] // 2

    def issue(expert, slot_, r0, count):
        def body(r, _):
            row = r0 + r
            tok = idx_ref[expert * cap + row]
            pltpu.make_async_copy(hn_hbm.at[pl.ds(tok, 1), :], buf.at[slot_, pl.ds(row, 1), :],
                                  sem.at[slot_]).start()
            return 0
        lax.fori_loop(0, count, body, 0, unroll=8)

    @pl.when((e == 0) & (f == 0))
    def _():
        issue(0, 0, 0, cap)

    @pl.when(f == 0)
    def _():
        pltpu.make_async_copy(buf.at[slot], buf.at[slot], sem.at[slot]).wait()
        o_ref[...] = jnp.zeros(o_ref.shape, F32)

    wgb[...] = wg_ref[...].astype(BF16)
    wub[...] = wu_ref[...].astype(BF16)
    wdb[...] = wd_ref[...].astype(BF16)

    nxt = jnp.minimum(e + 1, n_e - 1)
    n_chunks = cap // chunk
    per_chunk = cap // (n_f * n_chunks)
    first = f == 0
    last = f == n_f - 1

    for c in range(n_chunks):
        r_base = f * (cap // n_f) + c * per_chunk
        for r in range(per_chunk):
            row = r_base + r
            tok = idx_ref[nxt * cap + row]
            pltpu.make_async_copy(hn_hbm.at[pl.ds(tok, 1), :], buf.at[1 - slot, pl.ds(row, 1), :],
                                  sem.at[1 - slot]).start()
        rs = slice(c * chunk, (c + 1) * chunk)
        w = buf[slot, rs, :]
        x_lo = pltpu.bitcast(w << 16, F32).astype(BF16)
        x_hi = pltpu.bitcast(w & jnp.uint32(0xFFFF0000), F32).astype(BF16)
        gg = (jnp.dot(x_lo, wgb[:half, :], preferred_element_type=F32)
              + jnp.dot(x_hi, wgb[half:, :], preferred_element_type=F32))
        uu = (jnp.dot(x_lo, wub[:half, :], preferred_element_type=F32)
              + jnp.dot(x_hi, wub[half:, :], preferred_element_type=F32))
        hid = (jax.nn.silu(gg) * uu).astype(BF16)
        part = jnp.dot(hid, wdb[...], preferred_element_type=F32)
        o_ref[rs, :] = (o_ref[rs, :] + part) * jnp.where(last, gate_ref[rs, :], 1.0)

    @pl.when((e == n_e - 1) & last)
    def _():
        pltpu.make_async_copy(buf.at[1 - slot], buf.at[1 - slot], sem.at[1 - slot]).wait()


def _expert_ffn(idx_flat, hn_packed, gates_col, w_gate, w_up, w_down, layer, cap, tf=512, chunk=512):
    d = 2 * hn_packed.shape[1]
    n_f = D_EXPERT // tf
    assert cap % (n_f * (cap // chunk)) == 0 and cap % chunk == 0
    kern = functools.partial(_ffn_kernel, chunk=chunk, cap=cap, n_f=n_f)
    grid_spec = pltpu.PrefetchScalarGridSpec(
        num_scalar_prefetch=1,
        grid=(N_EXPERTS, n_f),
        in_specs=[
            pl.BlockSpec(memory_space=pl.ANY),
            pl.BlockSpec((cap, 1), lambda e, f, idx: (e, 0)),
            pl.BlockSpec((None, None, d, tf), lambda e, f, idx: (layer, e, 0, f)),
            pl.BlockSpec((None, None, d, tf), lambda e, f, idx: (layer, e, 0, f)),
            pl.BlockSpec((None, None, tf, d), lambda e, f, idx: (layer, e, f, 0)),
        ],
        out_specs=pl.BlockSpec((cap, d), lambda e, f, idx: (e, 0)),
        scratch_shapes=[pltpu.VMEM((2, cap, d // 2), jnp.uint32), pltpu.SemaphoreType.DMA((2,)),
                        pltpu.VMEM((d, tf), BF16), pltpu.VMEM((d, tf), BF16), pltpu.VMEM((tf, d), BF16)],
    )
    return pl.pallas_call(
        kern,
        grid_spec=grid_spec,
        out_shape=jax.ShapeDtypeStruct((N_EXPERTS * cap, d), F32),
        compiler_params=_cparams(("arbitrary", "arbitrary")),
        name="expert_ffn",
    )(idx_flat, hn_packed, gates_col, w_gate, w_up, w_down)


SEG = 64
SEG_AL = SEG + 8


def _combine_kernel(offs_ref, y_ref, pos_ref, jl_ref, rmat_ref, ye_hbm, o_ref, stag, sem, *,
                    cap, nbk, blocks_per_tile):
    i = pl.program_id(0)
    n_t = pl.num_programs(0)
    slot = i % 2
    tile_rows = y_ref.shape[0]

    def seg(tile, e, k):
        base = offs_ref[e * (nbk + 1) + tile * blocks_per_tile] + k * SEG
        src = jnp.minimum((base // 8) * 8, cap - SEG_AL)
        return base, src

    def issue(tile, k, slot_):
        for e in range(N_EXPERTS):
            _, src = seg(tile, e, k)
            pltpu.make_async_copy(ye_hbm.at[pl.ds(pl.multiple_of(e * cap + src, 8), SEG_AL), :],
                                  stag.at[slot_, pl.ds(e * SEG_AL, SEG_AL), :], sem.at[slot_]).start()

    def wait(slot_):
        pltpu.make_async_copy(stag.at[slot_], stag.at[slot_], sem.at[slot_]).wait()

    def contrib(k):
        pos = pos_ref[...]
        lane_e = lax.broadcasted_iota(jnp.int32, pos.shape, 1)
        base_v = jnp.zeros(pos.shape, jnp.int32)
        src_v = jnp.zeros(pos.shape, jnp.int32)
        for e in range(N_EXPERTS):
            base, src = seg(i, e, k)
            base_v = jnp.where(lane_e == e, base, base_v)
            src_v = jnp.where(lane_e == e, src, src_v)
        valid = (pos >= base_v) & (pos < base_v + SEG)
        rel = jnp.where(valid, pos - src_v, -1).astype(F32).astype(BF16)
        rep = jnp.dot(rel, rmat_ref[...], preferred_element_type=F32)
        onehot = jnp.where(rep == jl_ref[...], 1.0, 0.0).astype(BF16)
        st = stag[slot]
        hi = st.astype(BF16)
        lo = (st - hi.astype(F32)).astype(BF16)
        return (jnp.dot(onehot, hi, preferred_element_type=F32)
                + jnp.dot(onehot, lo, preferred_element_type=F32))

    @pl.when(i == 0)
    def _():
        issue(0, 0, 0)

    @pl.when(i + 1 < n_t)
    def _():
        issue(i + 1, 0, 1 - slot)

    wait(slot)
    acc = y_ref[...] + contrib(0)

    max_cnt = jnp.int32(0)
    for e in range(N_EXPERTS):
        o0 = offs_ref[e * (nbk + 1) + i * blocks_per_tile]
        o1 = offs_ref[e * (nbk + 1) + (i + 1) * blocks_per_tile]
        max_cnt = jnp.maximum(max_cnt, o1 - o0)
    n_rounds = (max_cnt + SEG - 1) // SEG

    def extra(k, acc_):
        issue(i, k, slot)
        wait(slot)
        return acc_ + contrib(k)

    o_ref[...] = lax.fori_loop(1, n_rounds, extra, acc)


def _combine(offs_flat, y2d, pos_t, ye, cap, nbk, tile=256):
    n, d = y2d.shape
    ktot = N_EXPERTS * SEG_AL
    lane = np.arange(ktot)
    jl = jnp.asarray((lane % SEG_AL)[None, :], F32)
    rmat = jnp.asarray((lane[None, :] // SEG_AL) == np.arange(N_EXPERTS)[:, None], BF16)
    kern = functools.partial(_combine_kernel, cap=cap, nbk=nbk, blocks_per_tile=tile // LANES)
    grid_spec = pltpu.PrefetchScalarGridSpec(
        num_scalar_prefetch=1,
        grid=(n // tile,),
        in_specs=[
            pl.BlockSpec((tile, d), lambda i, offs: (i, 0)),
            pl.BlockSpec((tile, N_EXPERTS), lambda i, offs: (i, 0)),
            pl.BlockSpec((1, ktot), lambda i, offs: (0, 0)),
            pl.BlockSpec((N_EXPERTS, ktot), lambda i, offs: (0, 0)),
            pl.BlockSpec(memory_space=pl.ANY),
        ],
        out_specs=pl.BlockSpec((tile, d), lambda i, offs: (i, 0)),
        scratch_shapes=[pltpu.VMEM((2, ktot, d), F32), pltpu.SemaphoreType.DMA((2,))],
    )
    return pl.pallas_call(
        kern,
        grid_spec=grid_spec,
        out_shape=jax.ShapeDtypeStruct((n, d), F32),
        compiler_params=_cparams(("arbitrary",)),
        name="combine",
    )(offs_flat, y2d, pos_t, jl, rmat, ye)


def _tile_gain(g64, n_heads, scale=1.0):
    return jnp.tile(g64.astype(F32) * scale, n_heads)


def _trunk(x, mem, p):
    batch, seq, d = x.shape
    n = batch * seq
    cap = EC_CAPACITY_FACTOR * n // N_EXPERTS
    cos_t, sin_t = _rope_tables(seq)
    cos_m, sin_m = cos_t[:N_MEM], sin_t[:N_MEM]
    qscale = HEAD_DIM ** -0.5 * LOG2E
    x2d = x.reshape(n, d)
    mem2d = mem.reshape(batch * N_MEM, d)
    ones_kv = jnp.ones((MEM_W,), F32)
    for i in range(DEPTH):
        j = i // 2
        xq_gain = _tile_gain(p["xq_norm"][i], N_MEM_HEADS, qscale)
        if i % 2 == 0:
            gain = jnp.concatenate([_tile_gain(p["q_norm_a"][j], N_SELF_HEADS, qscale),
                                    _tile_gain(p["k_norm_a"][j], N_KV_HEADS),
                                    jnp.ones((KV_W_A,), F32), xq_gain])
            mask = [True] * 8 + [False] * 2 + [True] * 2
            proj = _fused_proj(x2d, p["norm_mix"][i], p["w_in_a"][j].astype(BF16), gain, mask, 8,
                               cos_t, sin_t, seq)
            self_out = _gqa_attention(proj.reshape(batch, seq, IN_W_A), batch, seq)
            qx_block = (SELF_W + 2 * KV_W_A) // MEM_W
        else:
            gain = jnp.concatenate([_tile_gain(p["q_norm_b"][j], N_SELF_HEADS, qscale),
                                    _tile_gain(p["k_norm_b"][j], N_SELF_HEADS),
                                    jnp.ones((SELF_W,), F32), xq_gain])
            mask = [True] * 12 + [False] * 6 + [True] * 2
            proj = _fused_proj(x2d, p["norm_mix"][i], p["w_in_b"][j].astype(BF16), gain, mask, 0,
                               cos_t, sin_t, seq)
            self_out = _natten(proj.reshape(batch, seq, IN_W_B), _natten_bias(p["na_rpb"][j]), batch, seq)
            qx_block = 3 * SELF_W // MEM_W
        kv_gain = jnp.concatenate([_tile_gain(p["xk_norm"][i], N_MEM_HEADS), ones_kv])
        kv = _fused_proj(mem2d, p["norm_mem"][i], p["w_mem_kv"][i].astype(BF16), kv_gain,
                         [True, True, False, False], 0, cos_m, sin_m, N_MEM)
        x2d, hn, logits_t = _mix(x2d, self_out.reshape(n, SELF_W), proj, qx_block,
                                 kv.reshape(batch, N_MEM, 2 * MEM_W), p["w_o"][i].astype(BF16),
                                 p["norm_ffn"][i], p["w_router"][i].T, seq)
        nbk = n // LANES
        idx, gates, pos, offs = _route(logits_t.reshape(N_EXPERTS, nbk, LANES), cap)
        ye = _expert_ffn(idx.reshape(-1), hn, gates.reshape(-1, 1),
                         p["w_gate"], p["w_up"], p["w_down"], i, cap)
        offs_flat = jnp.concatenate([offs[:, :, 0], jnp.full((N_EXPERTS, 1), cap, jnp.int32)],
                                    axis=1).reshape(-1)
        x2d = _combine(offs_flat, x2d, pos.reshape(N_EXPERTS, n).T, ye, cap, nbk)
    return x2d.reshape(batch, seq, d)


def kernel(x_prompt, x_sample, mem_prompt, mem_sample, norm_mix, w_in_a, q_norm_a, k_norm_a, w_in_b, q_norm_b, k_norm_b, na_rpb, norm_mem, w_mem_kv, xq_norm, xk_norm, w_o, norm_ffn, w_router, w_gate, w_up, w_down):
    p = dict(norm_mix=norm_mix, w_in_a=w_in_a, q_norm_a=q_norm_a, k_norm_a=k_norm_a, w_in_b=w_in_b,
             q_norm_b=q_norm_b, k_norm_b=k_norm_b, na_rpb=na_rpb, norm_mem=norm_mem, w_mem_kv=w_mem_kv,
             xq_norm=xq_norm, xk_norm=xk_norm, w_o=w_o, norm_ffn=norm_ffn, w_router=w_router,
             w_gate=w_gate, w_up=w_up, w_down=w_down)
    return _trunk(x_prompt, mem_prompt, p), _trunk(x_sample, mem_sample, p)
```

```python
import functools
import math

import jax
import jax.numpy as jnp
import numpy as np
from jax import lax
from jax.experimental import pallas as pl
from jax.experimental.pallas import tpu as pltpu

F32 = jnp.float32
BF16 = jnp.bfloat16

D_MODEL = 1024
DEPTH = 4
GRID_W = 64
HEAD_DIM = 64
N_SELF_HEADS = 12
N_KV_HEADS = 4
N_MEM_HEADS = 4
N_MEM = 256
SELF_W = N_SELF_HEADS * HEAD_DIM
KV_W_A = N_KV_HEADS * HEAD_DIM
MEM_W = N_MEM_HEADS * HEAD_DIM
IN_W_A = SELF_W + 2 * KV_W_A + MEM_W
IN_W_B = 3 * SELF_W + MEM_W
ROW_WIN = 8
COL_WIN = 16
ROPE_THETA = 10000.0
ROPE_AXIS_DIM = HEAD_DIM // 2
N_EXPERTS = 16
EC_CAPACITY_FACTOR = 2
D_EXPERT = 2 * D_MODEL
EPS = 1e-6

LANES = 128
VMEM_LIMIT = 56 * 1024 * 1024
NEG_BIG = -1e30
LOG2E = math.log2(math.e)


def _cparams(sem):
    return pltpu.CompilerParams(dimension_semantics=sem, vmem_limit_bytes=VMEM_LIMIT)


def _swap16(y):
    lane = lax.broadcasted_iota(jnp.int32, y.shape, 1)
    up = pltpu.roll(y, 16, axis=1)
    dn = pltpu.roll(y, LANES - 16, axis=1)
    return jnp.where((lane & 16) != 0, up, dn)


def _proj_kernel(x_ref, g_ref, w_ref, gain_ref, cos_ref, sin_ref, bd_ref, o_ref, *,
                 n_norm_slabs_mask, n_rope_slabs):
    x = x_ref[...]
    ms = jnp.mean(x * x, axis=-1, keepdims=True)
    h = (x * lax.rsqrt(ms + EPS)) * g_ref[...]
    proj = jnp.dot(h.astype(BF16), w_ref[...], preferred_element_type=F32)
    n_slabs = proj.shape[1] // LANES
    bd = bd_ref[...]
    for s in range(n_slabs):
        y = proj[:, s * LANES:(s + 1) * LANES]
        if n_norm_slabs_mask[s]:
            sq = y * y
            hi = sq.astype(BF16)
            lo = (sq - hi.astype(F32)).astype(BF16)
            ss = (jnp.dot(hi, bd, preferred_element_type=F32)
                  + jnp.dot(lo, bd, preferred_element_type=F32))
            y = (y * lax.rsqrt(ss * (1.0 / HEAD_DIM) + EPS)) * gain_ref[:, s * LANES:(s + 1) * LANES]
        if s < n_rope_slabs:
            y = y * cos_ref[...] + _swap16(y) * sin_ref[...]
        o_ref[:, s * LANES:(s + 1) * LANES] = y.astype(o_ref.dtype)


def _fused_proj(x2d, g, w_bf16, gain_full, norm_mask, n_rope_slabs, cos_t, sin_t, seq, tm=256):
    n, d = x2d.shape
    w_out = w_bf16.shape[1]
    assert n % tm == 0 and seq % tm == 0
    blocks_per_seq = seq // tm
    bd = jnp.asarray(np.kron(np.eye(LANES // HEAD_DIM), np.ones((HEAD_DIM, HEAD_DIM))), BF16)
    kern = functools.partial(_proj_kernel, n_norm_slabs_mask=tuple(norm_mask),
                             n_rope_slabs=n_rope_slabs)
    return pl.pallas_call(
        kern,
        grid=(n // tm,),
        in_specs=[
            pl.BlockSpec((tm, d), lambda i: (i, 0)),
            pl.BlockSpec((1, d), lambda i: (0, 0)),
            pl.BlockSpec((d, w_out), lambda i: (0, 0)),
            pl.BlockSpec((1, w_out), lambda i: (0, 0)),
            pl.BlockSpec((tm, LANES), lambda i: (i % blocks_per_seq, 0)),
            pl.BlockSpec((tm, LANES), lambda i: (i % blocks_per_seq, 0)),
            pl.BlockSpec((LANES, LANES), lambda i: (0, 0)),
        ],
        out_specs=pl.BlockSpec((tm, w_out), lambda i: (i, 0)),
        out_shape=jax.ShapeDtypeStruct((n, w_out), BF16),
        compiler_params=_cparams(("parallel",)),
        name="fused_proj",
    )(x2d, g.reshape(1, d), w_bf16, gain_full.reshape(1, w_out), cos_t, sin_t, bd)


def _rope_tables(seq):
    t = jnp.arange(seq)
    row = (t // GRID_W).astype(F32)
    col = (t % GRID_W).astype(F32)
    inv = 1.0 / (ROPE_THETA ** (jnp.arange(0, ROPE_AXIS_DIM, 2, dtype=F32) / ROPE_AXIS_DIM))
    ar = row[:, None] * inv[None, :]
    ac = col[:, None] * inv[None, :]
    cr, sr, cc, sc = jnp.cos(ar), jnp.sin(ar), jnp.cos(ac), jnp.sin(ac)
    cos_h = jnp.concatenate([cr, cr, cc, cc], axis=-1)
    sin_h = jnp.concatenate([-sr, sr, -sc, sc], axis=-1)
    reps = LANES // HEAD_DIM
    return jnp.tile(cos_h, (1, reps)), jnp.tile(sin_h, (1, reps))


def _gqa_kernel(q_ref, k_ref, v_ref, o_ref, qs_ref, vx_ref, s_ref, p_ref, *, tk, rb):
    tq = q_ref.shape[0]
    seq = k_ref.shape[0]
    group = N_SELF_HEADS // N_KV_HEADS
    n_kv = seq // tk
    for g in range(N_KV_HEADS):
        for r in range(group):
            hh = g * group + r
            qs_ref[g, r * tq:(r + 1) * tq, :] = q_ref[:, hh * HEAD_DIM:(hh + 1) * HEAD_DIM]
    m_rows = group * tq

    @pl.when(pl.program_id(1) == 0)
    def _():
        lane = lax.broadcasted_iota(jnp.int32, (seq, LANES - HEAD_DIM), 1)
        ones_col = jnp.where(lane == 0, 1.0, 0.0).astype(BF16)
        for g in range(N_KV_HEADS):
            vx_ref[:, g * LANES:g * LANES + HEAD_DIM] = v_ref[:, g * HEAD_DIM:(g + 1) * HEAD_DIM]
            vx_ref[:, g * LANES + HEAD_DIM:(g + 1) * LANES] = ones_col

    def body(j, carry):
        start = pl.multiple_of(j * tk, tk)
        out = []
        for g in range(N_KV_HEADS):
            m_prev, acc_prev = carry[g]
            sb = g % s_ref.shape[0]
            k_t = k_ref[pl.ds(start, tk), g * HEAD_DIM:(g + 1) * HEAD_DIM]
            s_ref[sb] = lax.dot_general(qs_ref[g], k_t, (((1,), (1,)), ((), ())),
                                        preferred_element_type=F32)
            m_new = jnp.maximum(m_prev, jnp.max(s_ref[sb], axis=-1, keepdims=True))
            alpha = jnp.exp2(m_prev - m_new)
            for rc in range(m_rows // rb):
                rows = slice(rc * rb, (rc + 1) * rb)
                p_ref[sb, rows, :] = jnp.exp2(s_ref[sb, rows, :] - m_new[rows]).astype(BF16)
            pv = jnp.dot(p_ref[sb], vx_ref[pl.ds(start, tk), g * LANES:(g + 1) * LANES],
                         preferred_element_type=F32)
            out.append((m_new, alpha * acc_prev + pv))
        return tuple(out)

    init = tuple((jnp.full((m_rows, 1), NEG_BIG, F32), jnp.zeros((m_rows, LANES), F32))
                 for _ in range(N_KV_HEADS))
    final = lax.fori_loop(0, n_kv, body, init, unroll=min(n_kv, 4))
    for g in range(N_KV_HEADS):
        acc = final[g][1]
        o = acc[:, :HEAD_DIM] / acc[:, HEAD_DIM:HEAD_DIM + 1]
        for r in range(group):
            hh = g * group + r
            o_ref[:, hh * HEAD_DIM:(hh + 1) * HEAD_DIM] = o[r * tq:(r + 1) * tq].astype(o_ref.dtype)


def _gqa_attention(proj, batch, seq, tq=256, tk=1024, rb=32):
    kern = functools.partial(_gqa_kernel, tk=tk, rb=rb)
    m_rows = (N_SELF_HEADS // N_KV_HEADS) * tq
    return pl.pallas_call(
        kern,
        grid=(batch, seq // tq),
        in_specs=[
            pl.BlockSpec((None, tq, SELF_W), lambda b, i: (b, i, 0)),
            pl.BlockSpec((None, seq, KV_W_A), lambda b, i: (b, 0, SELF_W // KV_W_A)),
            pl.BlockSpec((None, seq, KV_W_A), lambda b, i: (b, 0, SELF_W // KV_W_A + 1)),
        ],
        out_specs=pl.BlockSpec((None, tq, SELF_W), lambda b, i: (b, i, 0)),
        out_shape=jax.ShapeDtypeStruct((batch, seq, SELF_W), BF16),
        scratch_shapes=[pltpu.VMEM((N_KV_HEADS, m_rows, HEAD_DIM), BF16),
                        pltpu.VMEM((seq, N_KV_HEADS * LANES), BF16),
                        pltpu.VMEM((2, m_rows, tk), F32),
                        pltpu.VMEM((2, m_rows, tk), BF16)],
        compiler_params=_cparams(("parallel", "arbitrary")),
        name="gqa_attention",
    )(proj, proj, proj)


NA_Q_ROWS = 8
NA_K_ROWS = NA_Q_ROWS + ROW_WIN


def _natten_kernel(q_ref, k_ref, v_ref, bias_ref, o_ref, s_ref, p_ref, *, rows, rb):
    i = pl.program_id(2)
    n_i = pl.num_programs(2)
    ws = jnp.clip(i * NA_Q_ROWS - ROW_WIN // 2, 0, rows - NA_K_ROWS)
    start = pl.multiple_of(ws * GRID_W, GRID_W)
    variant = jnp.where(i == 0, 0, jnp.where(i == n_i - 1, 2, 1))
    nq = NA_Q_ROWS * GRID_W
    nk = NA_K_ROWS * GRID_W
    lane = lax.broadcasted_iota(jnp.int32, (nk, LANES - HEAD_DIM), 1)
    ones_col = jnp.where(lane == 0, 1.0, 0.0).astype(BF16)
    heads = range(LANES // HEAD_DIM)
    hsl = [slice(h * HEAD_DIM, (h + 1) * HEAD_DIM) for h in heads]
    for h in heads:
        k_w = k_ref[pl.ds(start, nk), hsl[h]]
        s_ref[h] = (lax.dot_general(q_ref[:, hsl[h]], k_w, (((1,), (1,)), ((), ())),
                                    preferred_element_type=F32) + bias_ref[h, variant])
    for h in heads:
        m = jnp.max(s_ref[h], axis=-1, keepdims=True)
        for rc in range(nq // rb):
            rs = slice(rc * rb, (rc + 1) * rb)
            p_ref[h, rs, :] = jnp.exp2(s_ref[h, rs, :] - m[rs]).astype(BF16)
    for h in heads:
        vx = jnp.concatenate([v_ref[pl.ds(start, nk), hsl[h]], ones_col], axis=1)
        acc = jnp.dot(p_ref[h], vx, preferred_element_type=F32)
        o_ref[:, hsl[h]] = (acc[:, :HEAD_DIM] / acc[:, HEAD_DIM:HEAD_DIM + 1]).astype(o_ref.dtype)


def _natten_bias(rpb):
    cols = np.arange(GRID_W)
    col_start = np.clip(cols - COL_WIN // 2, 0, GRID_W - COL_WIN)
    cp = np.arange(GRID_W)
    col_ok = (cp[None, :] >= col_start[:, None]) & (cp[None, :] < col_start[:, None] + COL_WIN)
    dc = np.clip(cp[None, :] - cols[:, None] + (COL_WIN - 1), 0, 2 * COL_WIN - 2)
    t = rpb[:, :, dc] * LOG2E
    qi = np.arange(NA_Q_ROWS)[None, :, None]
    kw = np.arange(NA_K_ROWS)[None, None, :]
    delta = np.array([0, ROW_WIN // 2, ROW_WIN])[:, None, None]
    sr = np.clip(qi + delta - ROW_WIN // 2, 0, NA_K_ROWS - ROW_WIN)
    row_ok = (kw >= sr) & (kw < sr + ROW_WIN)
    dr = np.clip(kw - qi - delta + (ROW_WIN - 1), 0, 2 * ROW_WIN - 2)
    b = t[:, dr]
    ok = row_ok[:, :, :, None, None] & col_ok[None, None, None]
    b = jnp.where(jnp.asarray(ok)[None], b, NEG_BIG)
    b = b.transpose(0, 1, 2, 4, 3, 5)
    return b.reshape(b.shape[0], 3, NA_Q_ROWS * GRID_W, NA_K_ROWS * GRID_W).astype(F32)


def _natten(proj, bias, batch, seq, rb=32):
    rows = seq // GRID_W
    nq = NA_Q_ROWS * GRID_W
    nk = NA_K_ROWS * GRID_W
    hp = N_SELF_HEADS * HEAD_DIM // LANES
    hpb = LANES // HEAD_DIM
    assert rows % NA_Q_ROWS == 0 and rows // NA_Q_ROWS >= 2 and rows >= NA_K_ROWS
    kern = functools.partial(_natten_kernel, rows=rows, rb=rb)
    return pl.pallas_call(
        kern,
        grid=(hp, batch, rows // NA_Q_ROWS),
        in_specs=[
            pl.BlockSpec((None, nq, LANES), lambda h, b, i: (b, i, h)),
            pl.BlockSpec((None, seq, LANES), lambda h, b, i: (b, 0, hp + h)),
            pl.BlockSpec((None, seq, LANES), lambda h, b, i: (b, 0, 2 * hp + h)),
            pl.BlockSpec((hpb, 3, nq, nk), lambda h, b, i: (h, 0, 0, 0)),
        ],
        out_specs=pl.BlockSpec((None, nq, LANES), lambda h, b, i: (b, i, h)),
        out_shape=jax.ShapeDtypeStruct((batch, seq, SELF_W), BF16),
        scratch_shapes=[pltpu.VMEM((hpb, nq, nk), F32), pltpu.VMEM((hpb, nq, nk), BF16)],
        compiler_params=_cparams(("parallel", "parallel", "arbitrary")),
        name="natten",
    )(proj, proj, proj, bias)


def _split_bf16(a):
    hi = a.astype(BF16)
    lo = (a - hi.astype(F32)).astype(BF16)
    return hi, lo


def _mix_kernel(x_ref, so_ref, qx_ref, kbd_ref, vbd_ref, obd_ref, wo_ref, g_ref, wrh_ref, wrl_ref,
                xo_ref, hn_ref, lg_ref):
    s = jnp.dot(qx_ref[...], kbd_ref[...], preferred_element_type=F32)
    ps = []
    for h in range(N_MEM_HEADS):
        s_h = s[:, h * N_MEM:(h + 1) * N_MEM]
        ps.append(jnp.exp2(s_h - jnp.max(s_h, axis=-1, keepdims=True)).astype(BF16))
    p = jnp.concatenate(ps, axis=-1)
    mem_out = (jnp.dot(p, vbd_ref[...], preferred_element_type=F32)
               / jnp.dot(p, obd_ref[...], preferred_element_type=F32)).astype(BF16)
    y = (x_ref[...]
         + jnp.dot(so_ref[...], wo_ref[:SELF_W, :], preferred_element_type=F32)
         + jnp.dot(mem_out, wo_ref[SELF_W:, :], preferred_element_type=F32))
    xo_ref[...] = y
    ms = jnp.mean(y * y, axis=-1, keepdims=True)
    hn = (y * lax.rsqrt(ms + EPS)) * g_ref[...]
    bits = pltpu.bitcast(hn.astype(BF16).astype(F32), jnp.uint32)
    half = hn.shape[1] // 2
    hn_ref[...] = (bits[:, :half] >> 16) | bits[:, half:]
    hi, lo = _split_bf16(hn)
    dn = (((1,), (1,)), ((), ()))
    lg_ref[...] = (lax.dot_general(wrh_ref[...], hi, dn, preferred_element_type=F32)
                   + lax.dot_general(wrh_ref[...], lo, dn, preferred_element_type=F32)
                   + lax.dot_general(wrl_ref[...], hi, dn, preferred_element_type=F32))


def _mix(x2d, self_out2d, proj2d, qx_block, kv, wo_bf16, g_ffn, wr_t, seq, tm=512):
    n, d = x2d.shape
    blocks_per_seq = seq // tm
    wrh, wrl = _split_bf16(wr_t)
    b = kv.shape[0]
    eye = jnp.eye(N_MEM_HEADS, dtype=kv.dtype)
    km = kv[..., :MEM_W].reshape(b, N_MEM, N_MEM_HEADS, HEAD_DIM)
    vm = kv[..., MEM_W:].reshape(b, N_MEM, N_MEM_HEADS, HEAD_DIM)
    kbd = (km.transpose(0, 2, 3, 1)[:, :, :, None, :] * eye[None, :, None, :, None]
           ).reshape(b, MEM_W, N_MEM_HEADS * N_MEM)
    vbd = (vm.transpose(0, 2, 1, 3)[:, :, :, None, :] * eye[None, :, None, :, None]
           ).reshape(b, N_MEM_HEADS * N_MEM, MEM_W)
    obd = jnp.asarray(np.kron(np.eye(N_MEM_HEADS), np.ones((N_MEM, HEAD_DIM))), BF16)
    return pl.pallas_call(
        _mix_kernel,
        grid=(n // tm,),
        in_specs=[
            pl.BlockSpec((tm, d), lambda i: (i, 0)),
            pl.BlockSpec((tm, SELF_W), lambda i: (i, 0)),
            pl.BlockSpec((tm, MEM_W), lambda i: (i, qx_block)),
            pl.BlockSpec((None, MEM_W, N_MEM_HEADS * N_MEM), lambda i: (i // blocks_per_seq, 0, 0)),
            pl.BlockSpec((None, N_MEM_HEADS * N_MEM, MEM_W), lambda i: (i // blocks_per_seq, 0, 0)),
            pl.BlockSpec((N_MEM_HEADS * N_MEM, MEM_W), lambda i: (0, 0)),
            pl.BlockSpec((d, d), lambda i: (0, 0)),
            pl.BlockSpec((1, d), lambda i: (0, 0)),
            pl.BlockSpec((N_EXPERTS, d), lambda i: (0, 0)),
            pl.BlockSpec((N_EXPERTS, d), lambda i: (0, 0)),
        ],
        out_specs=[
            pl.BlockSpec((tm, d), lambda i: (i, 0)),
            pl.BlockSpec((tm, d // 2), lambda i: (i, 0)),
            pl.BlockSpec((N_EXPERTS, tm), lambda i: (0, i)),
        ],
        out_shape=[
            jax.ShapeDtypeStruct((n, d), F32),
            jax.ShapeDtypeStruct((n, d // 2), jnp.uint32),
            jax.ShapeDtypeStruct((N_EXPERTS, n), F32),
        ],
        compiler_params=_cparams(("parallel",)),
        name="mix_out_router",
    )(x2d, self_out2d, proj2d, kbd, vbd, obd, wo_bf16, g_ffn.reshape(1, d), wrh, wrl)


def _route_kernel(lg_ref, idx_ref, gate_ref, pos_ref, offs_ref, aff_s, gt_s, eq_s, need_s, *, cap):
    lg = lg_ref[...]
    n_e, nbk, _ = lg.shape
    ex = jnp.exp(lg - jnp.max(lg, axis=0, keepdims=True))
    aff = ex / jnp.sum(ex, axis=0, keepdims=True)
    keys = pltpu.bitcast(aff, jnp.int32)

    def count(mask_f):
        return jnp.sum(jnp.sum(mask_f, axis=1, keepdims=True), axis=2, keepdims=True)

    def bisect(it, thr):
        cand = thr | jnp.left_shift(jnp.int32(1), 30 - it)
        cnt = count(jnp.where(keys >= cand, 1.0, 0.0))
        return jnp.where(cnt >= float(cap), cand, thr)

    thr = lax.fori_loop(0, 31, bisect, jnp.zeros((n_e, 1, 1), jnp.int32))
    gt = jnp.where(keys > thr, 1.0, 0.0)
    eq = jnp.where(keys == thr, 1.0, 0.0)
    need = float(cap) - count(gt)
    aff_s[...] = aff
    gt_s[...] = gt
    eq_s[...] = eq
    need_s[...] = jnp.broadcast_to(need, need_s.shape)

    def tri(shape, fn):
        r = lax.broadcasted_iota(jnp.int32, shape, 0)
        c = lax.broadcasted_iota(jnp.int32, shape, 1)
        return jnp.where(fn(r, c), 1.0, 0.0).astype(BF16)

    u_lane = tri((LANES, LANES), lambda r, c: r <= c)
    l_blk = tri((nbk, nbk), lambda r, c: c < r)
    u_blk = tri((nbk, nbk), lambda r, c: r <= c)
    ones_l = jnp.ones((LANES, LANES), BF16)
    ones_b = jnp.ones((nbk, LANES), BF16)
    ones_8 = jnp.ones((8, LANES), BF16)
    dn_t = (((1,), (1,)), ((), ()))

    def cums(m):
        loc = jnp.dot(m.astype(BF16), u_lane, preferred_element_type=F32)
        tot = jnp.broadcast_to(loc[:, LANES - 1:LANES], (nbk, LANES))
        offs = jnp.dot(l_blk, tot.astype(BF16), preferred_element_type=F32)
        return loc, offs

    p_b = lax.broadcasted_iota(jnp.int32, (cap, nbk), 0).astype(F32)
    p_l = lax.broadcasted_iota(jnp.int32, (cap, LANES), 0).astype(F32)
    lane_b = lax.broadcasted_iota(jnp.int32, (cap, nbk), 1).astype(F32)
    lane_l = lax.broadcasted_iota(jnp.int32, (cap, LANES), 1).astype(F32)
    eye = (lax.broadcasted_iota(jnp.int32, (LANES, LANES), 0)
           == lax.broadcasted_iota(jnp.int32, (LANES, LANES), 1))

    def per_expert(e, _):
        eqm = eq_s[e]
        loc, offs = cums(eqm)
        tie_rank = loc + offs - eqm
        sel = gt_s[e] + eqm * jnp.where(tie_rank < need_s[e][0:1, 0:1], 1.0, 0.0)
        loc2, offs2 = cums(sel)
        pos_ref[e] = jnp.where(sel > 0.0, loc2 + offs2 - sel, -1.0).astype(jnp.int32)
        offs_ref[e] = offs2.astype(jnp.int32)
        tot_row = lax.dot_general(ones_8, sel.astype(BF16), dn_t, preferred_element_type=F32)
        s_row = jnp.dot(tot_row.astype(BF16), u_blk, preferred_element_type=F32)[0:1, :]
        nbv = jnp.dot(jnp.where(s_row <= p_b, 1.0, 0.0).astype(BF16), ones_b, preferred_element_type=F32)
        onehot = jnp.where(lane_b == nbv[:, :nbk], 1.0, 0.0).astype(BF16)
        g_loc = jnp.dot(onehot, loc2.astype(BF16), preferred_element_type=F32)
        o_hi = jnp.floor(offs2 * (1.0 / LANES))
        o_lo = offs2 - o_hi * LANES
        offp = (jnp.dot(onehot, o_hi.astype(BF16), preferred_element_type=F32) * LANES
                + jnp.dot(onehot, o_lo.astype(BF16), preferred_element_type=F32))
        il = jnp.dot(jnp.where(g_loc <= p_l - offp, 1.0, 0.0).astype(BF16), ones_l,
                     preferred_element_type=F32)
        tok = nbv * LANES + il
        a = aff_s[e]
        a1 = a.astype(BF16)
        r1 = a - a1.astype(F32)
        a2 = r1.astype(BF16)
        a3 = (r1 - a2.astype(F32)).astype(BF16)
        g_aff = (jnp.dot(onehot, a1, preferred_element_type=F32)
                 + jnp.dot(onehot, a2, preferred_element_type=F32)
                 + jnp.dot(onehot, a3, preferred_element_type=F32))
        gate_ref[e] = jnp.sum(jnp.where(lane_l == il, g_aff, 0.0), axis=1, keepdims=True)
        tok3 = tok.reshape(cap // LANES, LANES, LANES)
        idx_ref[e] = jnp.sum(jnp.where(eye[None], tok3, 0.0), axis=1).astype(jnp.int32)
        return 0

    lax.fori_loop(0, n_e, per_expert, 0)


def _route(logits_blk, cap):
    n_e, nbk, _ = logits_blk.shape
    return pl.pallas_call(
        functools.partial(_route_kernel, cap=cap),
        out_shape=[
            jax.ShapeDtypeStruct((n_e, cap // LANES, LANES), jnp.int32),
            jax.ShapeDtypeStruct((n_e, cap, 1), F32),
            jax.ShapeDtypeStruct((n_e, nbk, LANES), jnp.int32),
            jax.ShapeDtypeStruct((n_e, nbk, LANES), jnp.int32),
        ],
        scratch_shapes=[pltpu.VMEM((n_e, nbk, LANES), F32), pltpu.VMEM((n_e, nbk, LANES), F32),
                        pltpu.VMEM((n_e, nbk, LANES), F32), pltpu.VMEM((n_e, 8, LANES), F32)],
        compiler_params=pltpu.CompilerParams(vmem_limit_bytes=VMEM_LIMIT),
        name="route",
    )(logits_blk)


def _ffn_kernel(idx_ref, hn_hbm, gate_ref, wg_ref, wu_ref, wd_ref, o_ref, buf, sem, wgb, wub, wdb, *,
                chunk, cap, n_f):
    e = pl.program_id(0)
    f = pl.program_id(1)
    n_e = pl.num_programs(0)
    slot = e % 2
    half = wgb.shape[0] // 2

    def issue(expert, slot_, r0, count):
        def body(r, _):
            row = r0 + r
            tok = idx_ref[expert * cap + row]
            pltpu.make_async_copy(hn_hbm.at[pl.ds(tok, 1), :], buf.at[slot_, pl.ds(row, 1), :],
                                  sem.at[slot_]).start()
            return 0
        lax.fori_loop(0, count, body, 0, unroll=8)

    @pl.when((e == 0) & (f == 0))
    def _():
        issue(0, 0, 0, cap)

    @pl.when(f == 0)
    def _():
        pltpu.make_async_copy(buf.at[slot], buf.at[slot], sem.at[slot]).wait()
        o_ref[...] = jnp.zeros(o_ref.shape, F32)

    wgb[...] = wg_ref[...].astype(BF16)

    nxt = jnp.minimum(e + 1, n_e - 1)
    n_chunks = cap // chunk
    per_chunk = cap // (n_f * n_chunks)
    first = f == 0
    last = f == n_f - 1

    for c in range(n_chunks):
        r_base = f * (cap // n_f) + c * per_chunk
        for r in range(per_chunk):
            row = r_base + r
            tok = idx_ref[nxt * cap + row]
            pltpu.make_async_copy(hn_hbm.at[pl.ds(tok, 1), :], buf.at[1 - slot, pl.ds(row, 1), :],
                                  sem.at[1 - slot]).start()
        rs = slice(c * chunk, (c + 1) * chunk)
        w = buf[slot, rs, :]
        x_lo = pltpu.bitcast(w << 16, F32).astype(BF16)
        x_hi = pltpu.bitcast(w & jnp.uint32(0xFFFF0000), F32).astype(BF16)
        gg = (jnp.dot(x_lo, wgb[:half, :], preferred_element_type=F32)
              + jnp.dot(x_hi, wgb[half:, :], preferred_element_type=F32))
        if c == 0:
            wub[...] = wu_ref[...].astype(BF16)
        uu = (jnp.dot(x_lo, wub[:half, :], preferred_element_type=F32)
              + jnp.dot(x_hi, wub[half:, :], preferred_element_type=F32))
        if c == 0:
            wdb[...] = wd_ref[...].astype(BF16)
        hid = (jax.nn.silu(gg) * uu).astype(BF16)
        part = jnp.dot(hid, wdb[...], preferred_element_type=F32)
        o_ref[rs, :] = (o_ref[rs, :] + part) * jnp.where(last, gate_ref[rs, :], 1.0)

    @pl.when((e == n_e - 1) & last)
    def _():
        pltpu.make_async_copy(buf.at[1 - slot], buf.at[1 - slot], sem.at[1 - slot]).wait()


def _expert_ffn(idx_flat, hn_packed, gates_col, w_gate, w_up, w_down, layer, cap, tf=512, chunk=512):
    d = 2 * hn_packed.shape[1]
    n_f = D_EXPERT // tf
    assert cap % (n_f * (cap // chunk)) == 0 and cap % chunk == 0
    kern = functools.partial(_ffn_kernel, chunk=chunk, cap=cap, n_f=n_f)
    grid_spec = pltpu.PrefetchScalarGridSpec(
        num_scalar_prefetch=1,
        grid=(N_EXPERTS, n_f),
        in_specs=[
            pl.BlockSpec(memory_space=pl.ANY),
            pl.BlockSpec((cap, 1), lambda e, f, idx: (e, 0)),
            pl.BlockSpec((None, None, d, tf), lambda e, f, idx: (layer, e, 0, f)),
            pl.BlockSpec((None, None, d, tf), lambda e, f, idx: (layer, e, 0, f)),
            pl.BlockSpec((None, None, tf, d), lambda e, f, idx: (layer, e, f, 0)),
        ],
        out_specs=pl.BlockSpec((cap, d), lambda e, f, idx: (e, 0)),
        scratch_shapes=[pltpu.VMEM((2, cap, d // 2), jnp.uint32), pltpu.SemaphoreType.DMA((2,)),
                        pltpu.VMEM((d, tf), BF16), pltpu.VMEM((d, tf), BF16), pltpu.VMEM((tf, d), BF16)],
    )
    return pl.pallas_call(
        kern,
        grid_spec=grid_spec,
        out_shape=jax.ShapeDtypeStruct((N_EXPERTS * cap, d), F32),
        compiler_params=_cparams(("arbitrary", "arbitrary")),
        name="expert_ffn",
    )(idx_flat, hn_packed, gates_col, w_gate, w_up, w_down)


SEG = 64
SEG_AL = SEG + 8


def _combine_kernel(offs_ref, y_ref, pos_ref, jl_ref, rmat_ref, ye_hbm, o_ref, stag, sem, *,
                    cap, nbk, blocks_per_tile):
    i = pl.program_id(0)
    n_t = pl.num_programs(0)
    slot = i % 2
    tile_rows = y_ref.shape[0]

    def seg(tile, e, k):
        base = offs_ref[e * (nbk + 1) + tile * blocks_per_tile] + k * SEG
        src = jnp.minimum((base // 8) * 8, cap - SEG_AL)
        return base, src

    def issue(tile, k, slot_):
        for e in range(N_EXPERTS):
            _, src = seg(tile, e, k)
            pltpu.make_async_copy(ye_hbm.at[pl.ds(pl.multiple_of(e * cap + src, 8), SEG_AL), :],
                                  stag.at[slot_, pl.ds(e * SEG_AL, SEG_AL), :], sem.at[slot_]).start()

    def wait(slot_):
        pltpu.make_async_copy(stag.at[slot_], stag.at[slot_], sem.at[slot_]).wait()

    def contrib(k):
        pos = pos_ref[...]
        lane_e = lax.broadcasted_iota(jnp.int32, pos.shape, 1)
        base_v = jnp.zeros(pos.shape, jnp.int32)
        src_v = jnp.zeros(pos.shape, jnp.int32)
        for e in range(N_EXPERTS):
            base, src = seg(i, e, k)
            base_v = jnp.where(lane_e == e, base, base_v)
            src_v = jnp.where(lane_e == e, src, src_v)
        valid = (pos >= base_v) & (pos < base_v + SEG)
        rel = jnp.where(valid, pos - src_v, -1).astype(F32).astype(BF16)
        rep = jnp.dot(rel, rmat_ref[...], preferred_element_type=F32)
        onehot = jnp.where(rep == jl_ref[...], 1.0, 0.0).astype(BF16)
        st = stag[slot]
        hi = st.astype(BF16)
        lo = (st - hi.astype(F32)).astype(BF16)
        return (jnp.dot(onehot, hi, preferred_element_type=F32)
                + jnp.dot(onehot, lo, preferred_element_type=F32))

    @pl.when(i == 0)
    def _():
        issue(0, 0, 0)

    @pl.when(i + 1 < n_t)
    def _():
        issue(i + 1, 0, 1 - slot)

    wait(slot)
    acc = y_ref[...] + contrib(0)

    max_cnt = jnp.int32(0)
    for e in range(N_EXPERTS):
        o0 = offs_ref[e * (nbk + 1) + i * blocks_per_tile]
        o1 = offs_ref[e * (nbk + 1) + (i + 1) * blocks_per_tile]
        max_cnt = jnp.maximum(max_cnt, o1 - o0)
    n_rounds = (max_cnt + SEG - 1) // SEG

    def extra(k, acc_):
        issue(i, k, slot)
        wait(slot)
        return acc_ + contrib(k)

    o_ref[...] = lax.fori_loop(1, n_rounds, extra, acc)


def _combine(offs_flat, y2d, pos_t, ye, cap, nbk, tile=256):
    n, d = y2d.shape
    ktot = N_EXPERTS * SEG_AL
    lane = np.arange(ktot)
    jl = jnp.asarray((lane % SEG_AL)[None, :], F32)
    rmat = jnp.asarray((lane[None, :] // SEG_AL) == np.arange(N_EXPERTS)[:, None], BF16)
    kern = functools.partial(_combine_kernel, cap=cap, nbk=nbk, blocks_per_tile=tile // LANES)
    grid_spec = pltpu.PrefetchScalarGridSpec(
        num_scalar_prefetch=1,
        grid=(n // tile,),
        in_specs=[
            pl.BlockSpec((tile, d), lambda i, offs: (i, 0)),
            pl.BlockSpec((tile, N_EXPERTS), lambda i, offs: (i, 0)),
            pl.BlockSpec((1, ktot), lambda i, offs: (0, 0)),
            pl.BlockSpec((N_EXPERTS, ktot), lambda i, offs: (0, 0)),
            pl.BlockSpec(memory_space=pl.ANY),
        ],
        out_specs=pl.BlockSpec((tile, d), lambda i, offs: (i, 0)),
        scratch_shapes=[pltpu.VMEM((2, ktot, d), F32), pltpu.SemaphoreType.DMA((2,))],
    )
    return pl.pallas_call(
        kern,
        grid_spec=grid_spec,
        out_shape=jax.ShapeDtypeStruct((n, d), F32),
        compiler_params=_cparams(("arbitrary",)),
        name="combine",
    )(offs_flat, y2d, pos_t, jl, rmat, ye)


def _tile_gain(g64, n_heads, scale=1.0):
    return jnp.tile(g64.astype(F32) * scale, n_heads)


def _prepare(p):
    q = dict(p)
    for name in ("w_in_a", "w_in_b", "w_mem_kv", "w_o"):
        q[name] = p[name].astype(BF16)
    q["na_bias"] = [_natten_bias(p["na_rpb"][j]) for j in range(p["na_rpb"].shape[0])]
    q["prepared"] = True
    return q


def _trunk(x, mem, p):
    if "prepared" not in p:
        p = _prepare(p)
    batch, seq, d = x.shape
    n = batch * seq
    cap = EC_CAPACITY_FACTOR * n // N_EXPERTS
    cos_t, sin_t = _rope_tables(seq)
    cos_m, sin_m = cos_t[:N_MEM], sin_t[:N_MEM]
    qscale = HEAD_DIM ** -0.5 * LOG2E
    x2d = x.reshape(n, d)
    mem2d = mem.reshape(batch * N_MEM, d)
    ones_kv = jnp.ones((MEM_W,), F32)
    for i in range(DEPTH):
        j = i // 2
        xq_gain = _tile_gain(p["xq_norm"][i], N_MEM_HEADS, qscale)
        if i % 2 == 0:
            gain = jnp.concatenate([_tile_gain(p["q_norm_a"][j], N_SELF_HEADS, qscale),
                                    _tile_gain(p["k_norm_a"][j], N_KV_HEADS),
                                    jnp.ones((KV_W_A,), F32), xq_gain])
            mask = [True] * 8 + [False] * 2 + [True] * 2
            proj = _fused_proj(x2d, p["norm_mix"][i], p["w_in_a"][j], gain, mask, 8,
                               cos_t, sin_t, seq)
            self_out = _gqa_attention(proj.reshape(batch, seq, IN_W_A), batch, seq)
            qx_block = (SELF_W + 2 * KV_W_A) // MEM_W
        else:
            gain = jnp.concatenate([_tile_gain(p["q_norm_b"][j], N_SELF_HEADS, qscale),
                                    _tile_gain(p["k_norm_b"][j], N_SELF_HEADS),
                                    jnp.ones((SELF_W,), F32), xq_gain])
            mask = [True] * 12 + [False] * 6 + [True] * 2
            proj = _fused_proj(x2d, p["norm_mix"][i], p["w_in_b"][j], gain, mask, 0,
                               cos_t, sin_t, seq)
            self_out = _natten(proj.reshape(batch, seq, IN_W_B), p["na_bias"][j], batch, seq)
            qx_block = 3 * SELF_W // MEM_W
        kv_gain = jnp.concatenate([_tile_gain(p["xk_norm"][i], N_MEM_HEADS), ones_kv])
        kv = _fused_proj(mem2d, p["norm_mem"][i], p["w_mem_kv"][i], kv_gain,
                         [True, True, False, False], 0, cos_m, sin_m, N_MEM)
        x2d, hn, logits_t = _mix(x2d, self_out.reshape(n, SELF_W), proj, qx_block,
                                 kv.reshape(batch, N_MEM, 2 * MEM_W), p["w_o"][i],
                                 p["norm_ffn"][i], p["w_router"][i].T, seq)
        nbk = n // LANES
        idx, gates, pos, offs = _route(logits_t.reshape(N_EXPERTS, nbk, LANES), cap)
        ye = _expert_ffn(idx.reshape(-1), hn, gates.reshape(-1, 1),
                         p["w_gate"], p["w_up"], p["w_down"], i, cap)
        offs_flat = jnp.concatenate([offs[:, :, 0], jnp.full((N_EXPERTS, 1), cap, jnp.int32)],
                                    axis=1).reshape(-1)
        x2d = _combine(offs_flat, x2d, pos.reshape(N_EXPERTS, n).T, ye, cap, nbk)
    return x2d.reshape(batch, seq, d)


def kernel(x_prompt, x_sample, mem_prompt, mem_sample, norm_mix, w_in_a, q_norm_a, k_norm_a, w_in_b, q_norm_b, k_norm_b, na_rpb, norm_mem, w_mem_kv, xq_norm, xk_norm, w_o, norm_ffn, w_router, w_gate, w_up, w_down):
    p = dict(norm_mix=norm_mix, w_in_a=w_in_a, q_norm_a=q_norm_a, k_norm_a=k_norm_a, w_in_b=w_in_b,
             q_norm_b=q_norm_b, k_norm_b=k_norm_b, na_rpb=na_rpb, norm_mem=norm_mem, w_mem_kv=w_mem_kv,
             xq_norm=xq_norm, xk_norm=xk_norm, w_o=w_o, norm_ffn=norm_ffn, w_router=w_router,
             w_gate=w_gate, w_up=w_up, w_down=w_down)
    p = _prepare(p)
    return _trunk(x_prompt, mem_prompt, p), _trunk(x_sample, mem_sample, p)
```

```python
import functools
import math

import jax
import jax.numpy as jnp
import numpy as np
from jax import lax
from jax.experimental import pallas as pl
from jax.experimental.pallas import tpu as pltpu

F32 = jnp.float32
BF16 = jnp.bfloat16

D_MODEL = 1024
DEPTH = 4
GRID_W = 64
HEAD_DIM = 64
N_SELF_HEADS = 12
N_KV_HEADS = 4
N_MEM_HEADS = 4
N_MEM = 256
SELF_W = N_SELF_HEADS * HEAD_DIM
KV_W_A = N_KV_HEADS * HEAD_DIM
MEM_W = N_MEM_HEADS * HEAD_DIM
IN_W_A = SELF_W + 2 * KV_W_A + MEM_W
IN_W_B = 3 * SELF_W + MEM_W
ROW_WIN = 8
COL_WIN = 16
ROPE_THETA = 10000.0
ROPE_AXIS_DIM = HEAD_DIM // 2
N_EXPERTS = 16
EC_CAPACITY_FACTOR = 2
D_EXPERT = 2 * D_MODEL
EPS = 1e-6

LANES = 128
SLAB = 256
VMEM_LIMIT = 56 * 1024 * 1024
NEG_BIG = -1e30
LOG2E = math.log2(math.e)


def _cparams(sem):
    return pltpu.CompilerParams(dimension_semantics=sem, vmem_limit_bytes=VMEM_LIMIT)


def _swap16(y):
    lane = lax.broadcasted_iota(jnp.int32, y.shape, 1)
    up = pltpu.roll(y, 16, axis=1)
    dn = pltpu.roll(y, y.shape[1] - 16, axis=1)
    return jnp.where((lane & 16) != 0, up, dn)


def _proj_kernel(x_ref, g_ref, w_ref, gain_ref, cos_ref, sin_ref, bd_ref, o_ref, *,
                 n_norm_slabs_mask, n_rope_slabs):
    x = x_ref[...]
    ms = jnp.mean(x * x, axis=-1, keepdims=True)
    h = (x * lax.rsqrt(ms + EPS)) * g_ref[...]
    proj = jnp.dot(h.astype(BF16), w_ref[...], preferred_element_type=F32)
    n_slabs = proj.shape[1] // SLAB
    bd = bd_ref[...]
    for s in range(n_slabs):
        cols = slice(s * SLAB, (s + 1) * SLAB)
        y = proj[:, cols]
        if n_norm_slabs_mask[s]:
            sq = y * y
            hi = sq.astype(BF16)
            lo = (sq - hi.astype(F32)).astype(BF16)
            ss = (jnp.dot(hi, bd, preferred_element_type=F32)
                  + jnp.dot(lo, bd, preferred_element_type=F32))
            y = (y * lax.rsqrt(ss * (1.0 / HEAD_DIM) + EPS)) * gain_ref[:, cols]
        if s < n_rope_slabs:
            y = y * cos_ref[...] + _swap16(y) * sin_ref[...]
        o_ref[:, cols] = y.astype(o_ref.dtype)


def _fused_proj(x2d, g, w_bf16, gain_full, norm_mask, n_rope_slabs, cos_t, sin_t, seq, tm=256):
    n, d = x2d.shape
    w_out = w_bf16.shape[1]
    assert n % tm == 0 and seq % tm == 0 and w_out % SLAB == 0
    blocks_per_seq = seq // tm
    per = SLAB // LANES
    assert all(len(set(norm_mask[i:i + per])) == 1 for i in range(0, len(norm_mask), per))
    assert n_rope_slabs % per == 0
    bd = jnp.asarray(np.kron(np.eye(SLAB // HEAD_DIM), np.ones((HEAD_DIM, HEAD_DIM))), BF16)
    kern = functools.partial(_proj_kernel, n_norm_slabs_mask=tuple(norm_mask[::per]),
                             n_rope_slabs=n_rope_slabs // per)
    return pl.pallas_call(
        kern,
        grid=(n // tm,),
        in_specs=[
            pl.BlockSpec((tm, d), lambda i: (i, 0)),
            pl.BlockSpec((1, d), lambda i: (0, 0)),
            pl.BlockSpec((d, w_out), lambda i: (0, 0)),
            pl.BlockSpec((1, w_out), lambda i: (0, 0)),
            pl.BlockSpec((tm, SLAB), lambda i: (i % blocks_per_seq, 0)),
            pl.BlockSpec((tm, SLAB), lambda i: (i % blocks_per_seq, 0)),
            pl.BlockSpec((SLAB, SLAB), lambda i: (0, 0)),
        ],
        out_specs=pl.BlockSpec((tm, w_out), lambda i: (i, 0)),
        out_shape=jax.ShapeDtypeStruct((n, w_out), BF16),
        compiler_params=_cparams(("parallel",)),
        name="fused_proj",
    )(x2d, g.reshape(1, d), w_bf16, gain_full.reshape(1, w_out), cos_t, sin_t, bd)


def _rope_tables(seq):
    t = jnp.arange(seq)
    row = (t // GRID_W).astype(F32)
    col = (t % GRID_W).astype(F32)
    inv = 1.0 / (ROPE_THETA ** (jnp.arange(0, ROPE_AXIS_DIM, 2, dtype=F32) / ROPE_AXIS_DIM))
    ar = row[:, None] * inv[None, :]
    ac = col[:, None] * inv[None, :]
    cr, sr, cc, sc = jnp.cos(ar), jnp.sin(ar), jnp.cos(ac), jnp.sin(ac)
    cos_h = jnp.concatenate([cr, cr, cc, cc], axis=-1)
    sin_h = jnp.concatenate([-sr, sr, -sc, sc], axis=-1)
    reps = SLAB // HEAD_DIM
    return jnp.tile(cos_h, (1, reps)), jnp.tile(sin_h, (1, reps))


def _gqa_kernel(q_ref, k_ref, v_ref, o_ref, qs_ref, vx_ref, s_ref, p_ref, *, tk, rb):
    tq = q_ref.shape[0]
    seq = k_ref.shape[0]
    group = N_SELF_HEADS // N_KV_HEADS
    n_kv = seq // tk
    for g in range(N_KV_HEADS):
        for r in range(group):
            hh = g * group + r
            qs_ref[g, r * tq:(r + 1) * tq, :] = q_ref[:, hh * HEAD_DIM:(hh + 1) * HEAD_DIM]
    m_rows = group * tq

    @pl.when(pl.program_id(1) == 0)
    def _():
        lane = lax.broadcasted_iota(jnp.int32, (seq, LANES - HEAD_DIM), 1)
        ones_col = jnp.where(lane == 0, 1.0, 0.0).astype(BF16)
        for g in range(N_KV_HEADS):
            vx_ref[:, g * LANES:g * LANES + HEAD_DIM] = v_ref[:, g * HEAD_DIM:(g + 1) * HEAD_DIM]
            vx_ref[:, g * LANES + HEAD_DIM:(g + 1) * LANES] = ones_col

    def body(j, carry):
        start = pl.multiple_of(j * tk, tk)
        out = []
        for g in range(N_KV_HEADS):
            m_prev, acc_prev = carry[g]
            sb = g % s_ref.shape[0]
            k_t = k_ref[pl.ds(start, tk), g * HEAD_DIM:(g + 1) * HEAD_DIM]
            s_ref[sb] = lax.dot_general(qs_ref[g], k_t, (((1,), (1,)), ((), ())),
                                        preferred_element_type=F32)
            m_new = jnp.maximum(m_prev, jnp.max(s_ref[sb], axis=-1, keepdims=True))
            alpha = jnp.exp2(m_prev - m_new)
            for rc in range(m_rows // rb):
                rows = slice(rc * rb, (rc + 1) * rb)
                p_ref[sb, rows, :] = jnp.exp2(s_ref[sb, rows, :] - m_new[rows]).astype(BF16)
            pv = jnp.dot(p_ref[sb], vx_ref[pl.ds(start, tk), g * LANES:(g + 1) * LANES],
                         preferred_element_type=F32)
            out.append((m_new, alpha * acc_prev + pv))
        return tuple(out)

    init = tuple((jnp.full((m_rows, 1), NEG_BIG, F32), jnp.zeros((m_rows, LANES), F32))
                 for _ in range(N_KV_HEADS))
    final = lax.fori_loop(0, n_kv, body, init, unroll=min(n_kv, 4))
    for g in range(N_KV_HEADS):
        acc = final[g][1]
        o = acc[:, :HEAD_DIM] / acc[:, HEAD_DIM:HEAD_DIM + 1]
        for r in range(group):
            hh = g * group + r
            o_ref[:, hh * HEAD_DIM:(hh + 1) * HEAD_DIM] = o[r * tq:(r + 1) * tq].astype(o_ref.dtype)


def _gqa_attention(proj, batch, seq, tq=256, tk=1024, rb=32):
    kern = functools.partial(_gqa_kernel, tk=tk, rb=rb)
    m_rows = (N_SELF_HEADS // N_KV_HEADS) * tq
    return pl.pallas_call(
        kern,
        grid=(batch, seq // tq),
        in_specs=[
            pl.BlockSpec((None, tq, SELF_W), lambda b, i: (b, i, 0)),
            pl.BlockSpec((None, seq, KV_W_A), lambda b, i: (b, 0, SELF_W // KV_W_A)),
            pl.BlockSpec((None, seq, KV_W_A), lambda b, i: (b, 0, SELF_W // KV_W_A + 1)),
        ],
        out_specs=pl.BlockSpec((None, tq, SELF_W), lambda b, i: (b, i, 0)),
        out_shape=jax.ShapeDtypeStruct((batch, seq, SELF_W), BF16),
        scratch_shapes=[pltpu.VMEM((N_KV_HEADS, m_rows, HEAD_DIM), BF16),
                        pltpu.VMEM((seq, N_KV_HEADS * LANES), BF16),
                        pltpu.VMEM((2, m_rows, tk), F32),
                        pltpu.VMEM((2, m_rows, tk), BF16)],
        compiler_params=_cparams(("parallel", "arbitrary")),
        name="gqa_attention",
    )(proj, proj, proj)


NA_Q_ROWS = 8
NA_K_ROWS = NA_Q_ROWS + ROW_WIN


def _natten_kernel(q_ref, k_ref, v_ref, tbl_ref, o_ref, bias_ref, s_ref, p_ref, *, rows, rb):
    i = pl.program_id(2)
    n_i = pl.num_programs(2)

    @pl.when((pl.program_id(1) == 0) & (i == 0))
    def _():
        neg = jnp.full((GRID_W, GRID_W), NEG_BIG, F32)
        for h in range(LANES // HEAD_DIM):
            for v, delta in enumerate((0, ROW_WIN // 2, ROW_WIN)):
                for qi in range(NA_Q_ROWS):
                    sr = min(max(qi + delta - ROW_WIN // 2, 0), NA_K_ROWS - ROW_WIN)
                    for w0 in range(0, NA_K_ROWS, LANES // GRID_W):
                        blks = []
                        for w in range(w0, w0 + LANES // GRID_W):
                            ok = sr <= w < sr + ROW_WIN
                            blks.append(tbl_ref[h, w - qi - delta + ROW_WIN - 1] if ok else neg)
                        bias_ref[h, v, qi * GRID_W:(qi + 1) * GRID_W,
                                 w0 * GRID_W:w0 * GRID_W + LANES] = jnp.concatenate(blks, axis=1)
    ws = jnp.clip(i * NA_Q_ROWS - ROW_WIN // 2, 0, rows - NA_K_ROWS)
    start = pl.multiple_of(ws * GRID_W, GRID_W)
    variant = jnp.where(i == 0, 0, jnp.where(i == n_i - 1, 2, 1))
    nq = NA_Q_ROWS * GRID_W
    nk = NA_K_ROWS * GRID_W
    lane = lax.broadcasted_iota(jnp.int32, (nk, LANES - HEAD_DIM), 1)
    ones_col = jnp.where(lane == 0, 1.0, 0.0).astype(BF16)
    heads = range(LANES // HEAD_DIM)
    hsl = [slice(h * HEAD_DIM, (h + 1) * HEAD_DIM) for h in heads]
    for h in heads:
        k_w = k_ref[pl.ds(start, nk), hsl[h]]
        s_ref[h] = (lax.dot_general(q_ref[:, hsl[h]], k_w, (((1,), (1,)), ((), ())),
                                    preferred_element_type=F32) + bias_ref[h, variant])
    for h in heads:
        m = jnp.max(s_ref[h], axis=-1, keepdims=True)
        for rc in range(nq // rb):
            rs = slice(rc * rb, (rc + 1) * rb)
            p_ref[h, rs, :] = jnp.exp2(s_ref[h, rs, :] - m[rs]).astype(BF16)
    for h in heads:
        vx = jnp.concatenate([v_ref[pl.ds(start, nk), hsl[h]], ones_col], axis=1)
        acc = jnp.dot(p_ref[h], vx, preferred_element_type=F32)
        o_ref[:, hsl[h]] = (acc[:, :HEAD_DIM] / acc[:, HEAD_DIM:HEAD_DIM + 1]).astype(o_ref.dtype)


def _natten_bias(rpb):
    cols = np.arange(GRID_W)
    col_start = np.clip(cols - COL_WIN // 2, 0, GRID_W - COL_WIN)
    cp = np.arange(GRID_W)
    col_ok = (cp[None, :] >= col_start[:, None]) & (cp[None, :] < col_start[:, None] + COL_WIN)
    dc = np.clip(cp[None, :] - cols[:, None] + (COL_WIN - 1), 0, 2 * COL_WIN - 2)
    t = rpb[:, :, dc] * LOG2E
    return jnp.where(jnp.asarray(col_ok)[None, None], t, NEG_BIG).astype(F32)


def _natten(proj, bias, batch, seq, rb=32):
    rows = seq // GRID_W
    nq = NA_Q_ROWS * GRID_W
    nk = NA_K_ROWS * GRID_W
    hp = N_SELF_HEADS * HEAD_DIM // LANES
    hpb = LANES // HEAD_DIM
    assert rows % NA_Q_ROWS == 0 and rows // NA_Q_ROWS >= 2 and rows >= NA_K_ROWS
    kern = functools.partial(_natten_kernel, rows=rows, rb=rb)
    return pl.pallas_call(
        kern,
        grid=(hp, batch, rows // NA_Q_ROWS),
        in_specs=[
            pl.BlockSpec((None, nq, LANES), lambda h, b, i: (b, i, h)),
            pl.BlockSpec((None, seq, LANES), lambda h, b, i: (b, 0, hp + h)),
            pl.BlockSpec((None, seq, LANES), lambda h, b, i: (b, 0, 2 * hp + h)),
            pl.BlockSpec((hpb, 2 * ROW_WIN - 1, GRID_W, GRID_W), lambda h, b, i: (h, 0, 0, 0)),
        ],
        out_specs=pl.BlockSpec((None, nq, LANES), lambda h, b, i: (b, i, h)),
        out_shape=jax.ShapeDtypeStruct((batch, seq, SELF_W), BF16),
        scratch_shapes=[pltpu.VMEM((hpb, 3, nq, nk), F32),
                        pltpu.VMEM((hpb, nq, nk), F32), pltpu.VMEM((hpb, nq, nk), BF16)],
        compiler_params=_cparams(("parallel", "arbitrary", "arbitrary")),
        name="natten",
    )(proj, proj, proj, bias)


def _split_bf16(a):
    hi = a.astype(BF16)
    lo = (a - hi.astype(F32)).astype(BF16)
    return hi, lo


def _mix_kernel(x_ref, so_ref, qx_ref, kbd_ref, vbd_ref, obd_ref, wo_ref, g_ref, wrh_ref, wrl_ref,
                xo_ref, hn_ref, lg_ref):
    s = jnp.dot(qx_ref[...], kbd_ref[...], preferred_element_type=F32)
    ps = []
    for h in range(N_MEM_HEADS):
        s_h = s[:, h * N_MEM:(h + 1) * N_MEM]
        ps.append(jnp.exp2(s_h - jnp.max(s_h, axis=-1, keepdims=True)).astype(BF16))
    p = jnp.concatenate(ps, axis=-1)
    mem_out = (jnp.dot(p, vbd_ref[...], preferred_element_type=F32)
               / jnp.dot(p, obd_ref[...], preferred_element_type=F32)).astype(BF16)
    y = (x_ref[...]
         + jnp.dot(so_ref[...], wo_ref[:SELF_W, :], preferred_element_type=F32)
         + jnp.dot(mem_out, wo_ref[SELF_W:, :], preferred_element_type=F32))
    xo_ref[...] = y
    ms = jnp.mean(y * y, axis=-1, keepdims=True)
    hn = (y * lax.rsqrt(ms + EPS)) * g_ref[...]
    bits = pltpu.bitcast(hn.astype(BF16).astype(F32), jnp.uint32)
    half = hn.shape[1] // 2
    hn_ref[...] = (bits[:, :half] >> 16) | bits[:, half:]
    hi, lo = _split_bf16(hn)
    dn = (((1,), (1,)), ((), ()))
    lg_ref[...] = (lax.dot_general(wrh_ref[...], hi, dn, preferred_element_type=F32)
                   + lax.dot_general(wrh_ref[...], lo, dn, preferred_element_type=F32)
                   + lax.dot_general(wrl_ref[...], hi, dn, preferred_element_type=F32))


def _mix(x2d, self_out2d, proj2d, qx_block, kv, wo_bf16, g_ffn, wr_t, seq, tm=512):
    n, d = x2d.shape
    blocks_per_seq = seq // tm
    wrh, wrl = _split_bf16(wr_t)
    b = kv.shape[0]
    eye = jnp.eye(N_MEM_HEADS, dtype=kv.dtype)
    km = kv[..., :MEM_W].reshape(b, N_MEM, N_MEM_HEADS, HEAD_DIM)
    vm = kv[..., MEM_W:].reshape(b, N_MEM, N_MEM_HEADS, HEAD_DIM)
    kbd = (km.transpose(0, 2, 3, 1)[:, :, :, None, :] * eye[None, :, None, :, None]
           ).reshape(b, MEM_W, N_MEM_HEADS * N_MEM)
    vbd = (vm.transpose(0, 2, 1, 3)[:, :, :, None, :] * eye[None, :, None, :, None]
           ).reshape(b, N_MEM_HEADS * N_MEM, MEM_W)
    obd = jnp.asarray(np.kron(np.eye(N_MEM_HEADS), np.ones((N_MEM, HEAD_DIM))), BF16)
    return pl.pallas_call(
        _mix_kernel,
        grid=(n // tm,),
        in_specs=[
            pl.BlockSpec((tm, d), lambda i: (i, 0)),
            pl.BlockSpec((tm, SELF_W), lambda i: (i, 0)),
            pl.BlockSpec((tm, MEM_W), lambda i: (i, qx_block)),
            pl.BlockSpec((None, MEM_W, N_MEM_HEADS * N_MEM), lambda i: (i // blocks_per_seq, 0, 0)),
            pl.BlockSpec((None, N_MEM_HEADS * N_MEM, MEM_W), lambda i: (i // blocks_per_seq, 0, 0)),
            pl.BlockSpec((N_MEM_HEADS * N_MEM, MEM_W), lambda i: (0, 0)),
            pl.BlockSpec((d, d), lambda i: (0, 0)),
            pl.BlockSpec((1, d), lambda i: (0, 0)),
            pl.BlockSpec((N_EXPERTS, d), lambda i: (0, 0)),
            pl.BlockSpec((N_EXPERTS, d), lambda i: (0, 0)),
        ],
        out_specs=[
            pl.BlockSpec((tm, d), lambda i: (i, 0)),
            pl.BlockSpec((tm, d // 2), lambda i: (i, 0)),
            pl.BlockSpec((N_EXPERTS, tm), lambda i: (0, i)),
        ],
        out_shape=[
            jax.ShapeDtypeStruct((n, d), F32),
            jax.ShapeDtypeStruct((n, d // 2), jnp.uint32),
            jax.ShapeDtypeStruct((N_EXPERTS, n), F32),
        ],
        compiler_params=_cparams(("parallel",)),
        name="mix_out_router",
    )(x2d, self_out2d, proj2d, kbd, vbd, obd, wo_bf16, g_ffn.reshape(1, d), wrh, wrl)


def _route_kernel(lg_ref, idx_ref, gate_ref, pos_ref, offs_ref, aff_s, gt_s, eq_s, need_s, *, cap):
    lg = lg_ref[...]
    n_e, nbk, _ = lg.shape
    ex = jnp.exp(lg - jnp.max(lg, axis=0, keepdims=True))
    aff = ex / jnp.sum(ex, axis=0, keepdims=True)
    keys = pltpu.bitcast(aff, jnp.int32)

    def count(mask_f):
        return jnp.sum(jnp.sum(mask_f, axis=1, keepdims=True), axis=2, keepdims=True)

    def bisect(it, thr):
        cand = thr | jnp.left_shift(jnp.int32(1), 30 - it)
        cnt = count(jnp.where(keys >= cand, 1.0, 0.0))
        return jnp.where(cnt >= float(cap), cand, thr)

    thr = lax.fori_loop(0, 31, bisect, jnp.zeros((n_e, 1, 1), jnp.int32))
    gt = jnp.where(keys > thr, 1.0, 0.0)
    eq = jnp.where(keys == thr, 1.0, 0.0)
    need = float(cap) - count(gt)
    aff_s[...] = aff
    gt_s[...] = gt
    eq_s[...] = eq
    need_s[...] = jnp.broadcast_to(need, need_s.shape)

    def tri(shape, fn):
        r = lax.broadcasted_iota(jnp.int32, shape, 0)
        c = lax.broadcasted_iota(jnp.int32, shape, 1)
        return jnp.where(fn(r, c), 1.0, 0.0).astype(BF16)

    u_lane = tri((LANES, LANES), lambda r, c: r <= c)
    l_blk = tri((nbk, nbk), lambda r, c: c < r)
    u_blk = tri((nbk, nbk), lambda r, c: r <= c)
    ones_l = jnp.ones((LANES, LANES), BF16)
    ones_b = jnp.ones((nbk, LANES), BF16)
    ones_8 = jnp.ones((8, LANES), BF16)
    dn_t = (((1,), (1,)), ((), ()))

    def cums(m):
        loc = jnp.dot(m.astype(BF16), u_lane, preferred_element_type=F32)
        tot = jnp.broadcast_to(loc[:, LANES - 1:LANES], (nbk, LANES))
        offs = jnp.dot(l_blk, tot.astype(BF16), preferred_element_type=F32)
        return loc, offs

    p_b = lax.broadcasted_iota(jnp.int32, (cap, nbk), 0).astype(F32)
    p_l = lax.broadcasted_iota(jnp.int32, (cap, LANES), 0).astype(F32)
    lane_b = lax.broadcasted_iota(jnp.int32, (cap, nbk), 1).astype(F32)
    lane_l = lax.broadcasted_iota(jnp.int32, (cap, LANES), 1).astype(F32)
    eye = (lax.broadcasted_iota(jnp.int32, (LANES, LANES), 0)
           == lax.broadcasted_iota(jnp.int32, (LANES, LANES), 1))

    def per_expert(e, _):
        eqm = eq_s[e]
        loc, offs = cums(eqm)
        tie_rank = loc + offs - eqm
        sel = gt_s[e] + eqm * jnp.where(tie_rank < need_s[e][0:1, 0:1], 1.0, 0.0)
        loc2, offs2 = cums(sel)
        pos_ref[e] = jnp.where(sel > 0.0, loc2 + offs2 - sel, -1.0).astype(jnp.int32)
        offs_ref[e] = offs2.astype(jnp.int32)
        tot_row = lax.dot_general(ones_8, sel.astype(BF16), dn_t, preferred_element_type=F32)
        s_row = jnp.dot(tot_row.astype(BF16), u_blk, preferred_element_type=F32)[0:1, :]
        nbv = jnp.dot(jnp.where(s_row <= p_b, 1.0, 0.0).astype(BF16), ones_b, preferred_element_type=F32)
        onehot = jnp.where(lane_b == nbv[:, :nbk], 1.0, 0.0).astype(BF16)
        g_loc = jnp.dot(onehot, loc2.astype(BF16), preferred_element_type=F32)
        o_hi = jnp.floor(offs2 * (1.0 / LANES))
        o_lo = offs2 - o_hi * LANES
        offp = (jnp.dot(onehot, o_hi.astype(BF16), preferred_element_type=F32) * LANES
                + jnp.dot(onehot, o_lo.astype(BF16), preferred_element_type=F32))
        il = jnp.dot(jnp.where(g_loc <= p_l - offp, 1.0, 0.0).astype(BF16), ones_l,
                     preferred_element_type=F32)
        tok = nbv * LANES + il
        a = aff_s[e]
        a1 = a.astype(BF16)
        r1 = a - a1.astype(F32)
        a2 = r1.astype(BF16)
        a3 = (r1 - a2.astype(F32)).astype(BF16)
        g_aff = (jnp.dot(onehot, a1, preferred_element_type=F32)
                 + jnp.dot(onehot, a2, preferred_element_type=F32)
                 + jnp.dot(onehot, a3, preferred_element_type=F32))
        gate_ref[e] = jnp.sum(jnp.where(lane_l == il, g_aff, 0.0), axis=1, keepdims=True)
        tok3 = tok.reshape(cap // LANES, LANES, LANES)
        idx_ref[e] = jnp.sum(jnp.where(eye[None], tok3, 0.0), axis=1).astype(jnp.int32)
        return 0

    lax.fori_loop(0, n_e, per_expert, 0)


def _route(logits_blk, cap):
    n_e, nbk, _ = logits_blk.shape
    return pl.pallas_call(
        functools.partial(_route_kernel, cap=cap),
        out_shape=[
            jax.ShapeDtypeStruct((n_e, cap // LANES, LANES), jnp.int32),
            jax.ShapeDtypeStruct((n_e, cap, 1), F32),
            jax.ShapeDtypeStruct((n_e, nbk, LANES), jnp.int32),
            jax.ShapeDtypeStruct((n_e, nbk, LANES), jnp.int32),
        ],
        scratch_shapes=[pltpu.VMEM((n_e, nbk, LANES), F32), pltpu.VMEM((n_e, nbk, LANES), F32),
                        pltpu.VMEM((n_e, nbk, LANES), F32), pltpu.VMEM((n_e, 8, LANES), F32)],
        compiler_params=pltpu.CompilerParams(vmem_limit_bytes=VMEM_LIMIT),
        name="route",
    )(logits_blk)


def _ffn_kernel(idx_ref, hn_hbm, gate_ref, wg_ref, wu_ref, wd_ref, o_ref, buf, sem, wgb, wub, wdb, *,
                chunk, cap, n_f):
    e = pl.program_id(0)
    f = pl.program_id(1)
    n_e = pl.num_programs(0)
    slot = e % 2
    half = wgb.shape[0] // 2

    def issue(expert, slot_, r0, count):
        def body(r, _):
            row = r0 + r
            tok = idx_ref[expert * cap + row]
            pltpu.make_async_copy(hn_hbm.at[pl.ds(tok, 1), :], buf.at[slot_, pl.ds(row, 1), :],
                                  sem.at[slot_]).start()
            return 0
        lax.fori_loop(0, count, body, 0, unroll=8)

    @pl.when((e == 0) & (f == 0))
    def _():
        issue(0, 0, 0, cap)

    @pl.when(f == 0)
    def _():
        pltpu.make_async_copy(buf.at[slot], buf.at[slot], sem.at[slot]).wait()
        o_ref[...] = jnp.zeros(o_ref.shape, F32)

    wgb[...] = wg_ref[...].astype(BF16)

    nxt = jnp.minimum(e + 1, n_e - 1)
    n_chunks = cap // chunk
    per_step = cap // n_f
    quota = -(-per_step // max(n_chunks - 1, 1))
    issue_at = [min(c * quota, per_step) for c in range(n_chunks)] + [per_step]
    if n_chunks > 1:
        issue_at[n_chunks - 1] = per_step
    last = f == n_f - 1

    for c in range(n_chunks):
        rs = slice(c * chunk, (c + 1) * chunk)
        w = buf[slot, rs, :]
        x_lo = pltpu.bitcast(w << 16, F32).astype(BF16)
        x_hi = pltpu.bitcast(w & jnp.uint32(0xFFFF0000), F32).astype(BF16)
        gg = (jnp.dot(x_lo, wgb[:half, :], preferred_element_type=F32)
              + jnp.dot(x_hi, wgb[half:, :], preferred_element_type=F32))
        for r in range(issue_at[c], issue_at[c + 1]):
            row = f * per_step + r
            tok = idx_ref[nxt * cap + row]
            pltpu.make_async_copy(hn_hbm.at[pl.ds(tok, 1), :], buf.at[1 - slot, pl.ds(row, 1), :],
                                  sem.at[1 - slot]).start()
        if c == 0:
            wub[...] = wu_ref[...].astype(BF16)
        uu = (jnp.dot(x_lo, wub[:half, :], preferred_element_type=F32)
              + jnp.dot(x_hi, wub[half:, :], preferred_element_type=F32))
        if c == 0:
            wdb[...] = wd_ref[...].astype(BF16)
        hid = (jax.nn.silu(gg) * uu).astype(BF16)
        part = jnp.dot(hid, wdb[...], preferred_element_type=F32)
        o_ref[rs, :] = (o_ref[rs, :] + part) * jnp.where(last, gate_ref[rs, :], 1.0)

    @pl.when((e == n_e - 1) & last)
    def _():
        pltpu.make_async_copy(buf.at[1 - slot], buf.at[1 - slot], sem.at[1 - slot]).wait()


def _expert_ffn(idx_flat, hn_packed, gates_col, w_gate, w_up, w_down, layer, cap, tf=512, chunk=512):
    d = 2 * hn_packed.shape[1]
    n_f = D_EXPERT // tf
    assert cap % (n_f * (cap // chunk)) == 0 and cap % chunk == 0
    kern = functools.partial(_ffn_kernel, chunk=chunk, cap=cap, n_f=n_f)
    grid_spec = pltpu.PrefetchScalarGridSpec(
        num_scalar_prefetch=1,
        grid=(N_EXPERTS, n_f),
        in_specs=[
            pl.BlockSpec(memory_space=pl.ANY),
            pl.BlockSpec((cap, 1), lambda e, f, idx: (e, 0)),
            pl.BlockSpec((None, None, d, tf), lambda e, f, idx: (layer, e, 0, f)),
            pl.BlockSpec((None, None, d, tf), lambda e, f, idx: (layer, e, 0, f)),
            pl.BlockSpec((None, None, tf, d), lambda e, f, idx: (layer, e, f, 0)),
        ],
        out_specs=pl.BlockSpec((cap, d), lambda e, f, idx: (e, 0)),
        scratch_shapes=[pltpu.VMEM((2, cap, d // 2), jnp.uint32), pltpu.SemaphoreType.DMA((2,)),
                        pltpu.VMEM((d, tf), BF16), pltpu.VMEM((d, tf), BF16), pltpu.VMEM((tf, d), BF16)],
    )
    return pl.pallas_call(
        kern,
        grid_spec=grid_spec,
        out_shape=jax.ShapeDtypeStruct((N_EXPERTS * cap, d), F32),
        compiler_params=_cparams(("arbitrary", "arbitrary")),
        name="expert_ffn",
    )(idx_flat, hn_packed, gates_col, w_gate, w_up, w_down)


SEG = 48
SEG_AL = SEG + 8


def _combine_kernel(offs_ref, y_ref, pos_ref, jl_ref, rmat_ref, ye_hbm, o_ref, stag, sem, *,
                    cap, nbk, blocks_per_tile):
    i = pl.program_id(0)
    n_t = pl.num_programs(0)
    slot = i % 2
    tile_rows = y_ref.shape[0]

    def seg(tile, e, k):
        base = offs_ref[e * (nbk + 1) + tile * blocks_per_tile] + k * SEG
        src = jnp.minimum((base // 8) * 8, cap - SEG_AL)
        return base, src

    def issue(tile, k, slot_):
        for e in range(N_EXPERTS):
            _, src = seg(tile, e, k)
            pltpu.make_async_copy(ye_hbm.at[pl.ds(pl.multiple_of(e * cap + src, 8), SEG_AL), :],
                                  stag.at[slot_, pl.ds(e * SEG_AL, SEG_AL), :], sem.at[slot_]).start()

    def wait(slot_):
        pltpu.make_async_copy(stag.at[slot_], stag.at[slot_], sem.at[slot_]).wait()

    def contrib(k):
        pos = pos_ref[...]
        lane_e = lax.broadcasted_iota(jnp.int32, pos.shape, 1)
        base_v = jnp.zeros(pos.shape, jnp.int32)
        src_v = jnp.zeros(pos.shape, jnp.int32)
        for e in range(N_EXPERTS):
            base, src = seg(i, e, k)
            base_v = jnp.where(lane_e == e, base, base_v)
            src_v = jnp.where(lane_e == e, src, src_v)
        valid = (pos >= base_v) & (pos < base_v + SEG)
        rel = jnp.where(valid, pos - src_v, -1).astype(F32).astype(BF16)
        rep = jnp.dot(rel, rmat_ref[...], preferred_element_type=F32)
        onehot = jnp.where(rep == jl_ref[...], 1.0, 0.0).astype(BF16)
        st = stag[slot]
        hi = st.astype(BF16)
        lo = (st - hi.astype(F32)).astype(BF16)
        return (jnp.dot(onehot, hi, preferred_element_type=F32)
                + jnp.dot(onehot, lo, preferred_element_type=F32))

    @pl.when(i == 0)
    def _():
        issue(0, 0, 0)

    @pl.when(i + 1 < n_t)
    def _():
        issue(i + 1, 0, 1 - slot)

    wait(slot)
    acc = y_ref[...] + contrib(0)

    max_cnt = jnp.int32(0)
    for e in range(N_EXPERTS):
        o0 = offs_ref[e * (nbk + 1) + i * blocks_per_tile]
        o1 = offs_ref[e * (nbk + 1) + (i + 1) * blocks_per_tile]
        max_cnt = jnp.maximum(max_cnt, o1 - o0)
    n_rounds = (max_cnt + SEG - 1) // SEG

    def extra(k, acc_):
        issue(i, k, slot)
        wait(slot)
        return acc_ + contrib(k)

    o_ref[...] = lax.fori_loop(1, n_rounds, extra, acc)


def _combine(offs_flat, y2d, pos_t, ye, cap, nbk, tile=256):
    n, d = y2d.shape
    ktot = N_EXPERTS * SEG_AL
    lane = np.arange(ktot)
    jl = jnp.asarray((lane % SEG_AL)[None, :], F32)
    rmat = jnp.asarray((lane[None, :] // SEG_AL) == np.arange(N_EXPERTS)[:, None], BF16)
    kern = functools.partial(_combine_kernel, cap=cap, nbk=nbk, blocks_per_tile=tile // LANES)
    grid_spec = pltpu.PrefetchScalarGridSpec(
        num_scalar_prefetch=1,
        grid=(n // tile,),
        in_specs=[
            pl.BlockSpec((tile, d), lambda i, offs: (i, 0)),
            pl.BlockSpec((tile, N_EXPERTS), lambda i, offs: (i, 0)),
            pl.BlockSpec((1, ktot), lambda i, offs: (0, 0)),
            pl.BlockSpec((N_EXPERTS, ktot), lambda i, offs: (0, 0)),
            pl.BlockSpec(memory_space=pl.ANY),
        ],
        out_specs=pl.BlockSpec((tile, d), lambda i, offs: (i, 0)),
        scratch_shapes=[pltpu.VMEM((2, ktot, d), F32), pltpu.SemaphoreType.DMA((2,))],
    )
    return pl.pallas_call(
        kern,
        grid_spec=grid_spec,
        out_shape=jax.ShapeDtypeStruct((n, d), F32),
        compiler_params=_cparams(("arbitrary",)),
        name="combine",
    )(offs_flat, y2d, pos_t, jl, rmat, ye)


def _tile_gain(g64, n_heads, scale=1.0):
    return jnp.tile(g64.astype(F32) * scale, n_heads)


def _prepare(p):
    q = dict(p)
    for name in ("w_in_a", "w_in_b", "w_mem_kv", "w_o"):
        q[name] = p[name].astype(BF16)
    q["na_bias"] = [_natten_bias(p["na_rpb"][j]) for j in range(p["na_rpb"].shape[0])]
    q["prepared"] = True
    return q


def _trunk(x, mem, p):
    if "prepared" not in p:
        p = _prepare(p)
    batch, seq, d = x.shape
    n = batch * seq
    cap = EC_CAPACITY_FACTOR * n // N_EXPERTS
    cos_t, sin_t = _rope_tables(seq)
    cos_m, sin_m = cos_t[:N_MEM], sin_t[:N_MEM]
    qscale = HEAD_DIM ** -0.5 * LOG2E
    x2d = x.reshape(n, d)
    mem2d = mem.reshape(batch * N_MEM, d)
    ones_kv = jnp.ones((MEM_W,), F32)
    for i in range(DEPTH):
        j = i // 2
        xq_gain = _tile_gain(p["xq_norm"][i], N_MEM_HEADS, qscale)
        if i % 2 == 0:
            gain = jnp.concatenate([_tile_gain(p["q_norm_a"][j], N_SELF_HEADS, qscale),
                                    _tile_gain(p["k_norm_a"][j], N_KV_HEADS),
                                    jnp.ones((KV_W_A,), F32), xq_gain])
            mask = [True] * 8 + [False] * 2 + [True] * 2
            proj = _fused_proj(x2d, p["norm_mix"][i], p["w_in_a"][j], gain, mask, 8,
                               cos_t, sin_t, seq, tm=512)
            self_out = _gqa_attention(proj.reshape(batch, seq, IN_W_A), batch, seq)
            qx_block = (SELF_W + 2 * KV_W_A) // MEM_W
        else:
            gain = jnp.concatenate([_tile_gain(p["q_norm_b"][j], N_SELF_HEADS, qscale),
                                    _tile_gain(p["k_norm_b"][j], N_SELF_HEADS),
                                    jnp.ones((SELF_W,), F32), xq_gain])
            mask = [True] * 12 + [False] * 6 + [True] * 2
            proj = _fused_proj(x2d, p["norm_mix"][i], p["w_in_b"][j], gain, mask, 0,
                               cos_t, sin_t, seq, tm=512)
            self_out = _natten(proj.reshape(batch, seq, IN_W_B), p["na_bias"][j], batch, seq)
            qx_block = 3 * SELF_W // MEM_W
        kv_gain = jnp.concatenate([_tile_gain(p["xk_norm"][i], N_MEM_HEADS), ones_kv])
        kv = _fused_proj(mem2d, p["norm_mem"][i], p["w_mem_kv"][i], kv_gain,
                         [True, True, False, False], 0, cos_m, sin_m, N_MEM)
        x2d, hn, logits_t = _mix(x2d, self_out.reshape(n, SELF_W), proj, qx_block,
                                 kv.reshape(batch, N_MEM, 2 * MEM_W), p["w_o"][i],
                                 p["norm_ffn"][i], p["w_router"][i].T, seq)
        nbk = n // LANES
        idx, gates, pos, offs = _route(logits_t.reshape(N_EXPERTS, nbk, LANES), cap)
        ye = _expert_ffn(idx.reshape(-1), hn, gates.reshape(-1, 1),
                         p["w_gate"], p["w_up"], p["w_down"], i, cap)
        offs_flat = jnp.concatenate([offs[:, :, 0], jnp.full((N_EXPERTS, 1), cap, jnp.int32)],
                                    axis=1).reshape(-1)
        x2d = _combine(offs_flat, x2d, pos.reshape(N_EXPERTS, n).T, ye, cap, nbk)
    return x2d.reshape(batch, seq, d)


def kernel(x_prompt, x_sample, mem_prompt, mem_sample, norm_mix, w_in_a, q_norm_a, k_norm_a, w_in_b, q_norm_b, k_norm_b, na_rpb, norm_mem, w_mem_kv, xq_norm, xk_norm, w_o, norm_ffn, w_router, w_gate, w_up, w_down):
    p = dict(norm_mix=norm_mix, w_in_a=w_in_a, q_norm_a=q_norm_a, k_norm_a=k_norm_a, w_in_b=w_in_b,
             q_norm_b=q_norm_b, k_norm_b=k_norm_b, na_rpb=na_rpb, norm_mem=norm_mem, w_mem_kv=w_mem_kv,
             xq_norm=xq_norm, xk_norm=xk_norm, w_o=w_o, norm_ffn=norm_ffn, w_router=w_router,
             w_gate=w_gate, w_up=w_up, w_down=w_down)
    p = _prepare(p)
    return _trunk(x_prompt, mem_prompt, p), _trunk(x_sample, mem_sample, p)
```

```python
import functools
import math

import jax
import jax.numpy as jnp
import numpy as np
from jax import lax
from jax.experimental import pallas as pl
from jax.experimental.pallas import tpu as pltpu

F32 = jnp.float32
BF16 = jnp.bfloat16

D_MODEL = 1024
DEPTH = 4
GRID_W = 64
HEAD_DIM = 64
N_SELF_HEADS = 12
N_KV_HEADS = 4
N_MEM_HEADS = 4
N_MEM = 256
SELF_W = N_SELF_HEADS * HEAD_DIM
KV_W_A = N_KV_HEADS * HEAD_DIM
MEM_W = N_MEM_HEADS * HEAD_DIM
IN_W_A = SELF_W + 2 * KV_W_A + MEM_W
IN_W_B = 3 * SELF_W + MEM_W
ROW_WIN = 8
COL_WIN = 16
ROPE_THETA = 10000.0
ROPE_AXIS_DIM = HEAD_DIM // 2
N_EXPERTS = 16
EC_CAPACITY_FACTOR = 2
D_EXPERT = 2 * D_MODEL
EPS = 1e-6

LANES = 128
SLAB = 256
VMEM_LIMIT = 56 * 1024 * 1024
NEG_BIG = -1e30
LOG2E = math.log2(math.e)


def _cparams(sem):
    return pltpu.CompilerParams(dimension_semantics=sem, vmem_limit_bytes=VMEM_LIMIT)


def _swap16(y):
    lane = lax.broadcasted_iota(jnp.int32, y.shape, 1)
    up = pltpu.roll(y, 16, axis=1)
    dn = pltpu.roll(y, y.shape[1] - 16, axis=1)
    return jnp.where((lane & 16) != 0, up, dn)


def _proj_kernel(x_ref, g_ref, w_ref, gain_ref, cos_ref, sin_ref, bd_ref, o_ref, *,
                 n_norm_slabs_mask, n_rope_slabs):
    x = x_ref[...]
    ms = jnp.mean(x * x, axis=-1, keepdims=True)
    h = (x * lax.rsqrt(ms + EPS)) * g_ref[...]
    proj = jnp.dot(h.astype(BF16), w_ref[...], preferred_element_type=F32)
    n_slabs = proj.shape[1] // SLAB
    bd = bd_ref[...]
    for s in range(n_slabs):
        cols = slice(s * SLAB, (s + 1) * SLAB)
        y = proj[:, cols]
        if n_norm_slabs_mask[s]:
            sq = y * y
            hi = sq.astype(BF16)
            lo = (sq - hi.astype(F32)).astype(BF16)
            ss = (jnp.dot(hi, bd, preferred_element_type=F32)
                  + jnp.dot(lo, bd, preferred_element_type=F32))
            y = (y * lax.rsqrt(ss * (1.0 / HEAD_DIM) + EPS)) * gain_ref[:, cols]
        if s < n_rope_slabs:
            y = y * cos_ref[...] + _swap16(y) * sin_ref[...]
        o_ref[:, cols] = y.astype(o_ref.dtype)


def _fused_proj(x2d, g, w_bf16, gain_full, norm_mask, n_rope_slabs, cos_t, sin_t, seq, tm=256):
    n, d = x2d.shape
    w_out = w_bf16.shape[1]
    assert n % tm == 0 and seq % tm == 0 and w_out % SLAB == 0
    blocks_per_seq = seq // tm
    per = SLAB // LANES
    assert all(len(set(norm_mask[i:i + per])) == 1 for i in range(0, len(norm_mask), per))
    assert n_rope_slabs % per == 0
    bd = jnp.asarray(np.kron(np.eye(SLAB // HEAD_DIM), np.ones((HEAD_DIM, HEAD_DIM))), BF16)
    kern = functools.partial(_proj_kernel, n_norm_slabs_mask=tuple(norm_mask[::per]),
                             n_rope_slabs=n_rope_slabs // per)
    return pl.pallas_call(
        kern,
        grid=(n // tm,),
        in_specs=[
            pl.BlockSpec((tm, d), lambda i: (i, 0)),
            pl.BlockSpec((1, d), lambda i: (0, 0)),
            pl.BlockSpec((d, w_out), lambda i: (0, 0)),
            pl.BlockSpec((1, w_out), lambda i: (0, 0)),
            pl.BlockSpec((tm, SLAB), lambda i: (i % blocks_per_seq, 0)),
            pl.BlockSpec((tm, SLAB), lambda i: (i % blocks_per_seq, 0)),
            pl.BlockSpec((SLAB, SLAB), lambda i: (0, 0)),
        ],
        out_specs=pl.BlockSpec((tm, w_out), lambda i: (i, 0)),
        out_shape=jax.ShapeDtypeStruct((n, w_out), BF16),
        compiler_params=_cparams(("parallel",)),
        name="fused_proj",
    )(x2d, g.reshape(1, d), w_bf16, gain_full.reshape(1, w_out), cos_t, sin_t, bd)


def _rope_tables(seq):
    t = jnp.arange(seq)
    row = (t // GRID_W).astype(F32)
    col = (t % GRID_W).astype(F32)
    inv = 1.0 / (ROPE_THETA ** (jnp.arange(0, ROPE_AXIS_DIM, 2, dtype=F32) / ROPE_AXIS_DIM))
    ar = row[:, None] * inv[None, :]
    ac = col[:, None] * inv[None, :]
    cr, sr, cc, sc = jnp.cos(ar), jnp.sin(ar), jnp.cos(ac), jnp.sin(ac)
    cos_h = jnp.concatenate([cr, cr, cc, cc], axis=-1)
    sin_h = jnp.concatenate([-sr, sr, -sc, sc], axis=-1)
    reps = SLAB // HEAD_DIM
    return jnp.tile(cos_h, (1, reps)), jnp.tile(sin_h, (1, reps))


def _gqa_kernel(q_ref, k_ref, v_ref, o_ref, qs_ref, vx_ref, s_ref, p_ref, *, tk, rb):
    tq = q_ref.shape[0]
    seq = k_ref.shape[0]
    group = N_SELF_HEADS // N_KV_HEADS
    n_kv = seq // tk
    for g in range(N_KV_HEADS):
        for r in range(group):
            hh = g * group + r
            qs_ref[g, r * tq:(r + 1) * tq, :] = q_ref[:, hh * HEAD_DIM:(hh + 1) * HEAD_DIM]
    m_rows = group * tq

    @pl.when(pl.program_id(1) == 0)
    def _():
        lane = lax.broadcasted_iota(jnp.int32, (seq, LANES - HEAD_DIM), 1)
        ones_col = jnp.where(lane == 0, 1.0, 0.0).astype(BF16)
        for g in range(N_KV_HEADS):
            vx_ref[:, g * LANES:g * LANES + HEAD_DIM] = v_ref[:, g * HEAD_DIM:(g + 1) * HEAD_DIM]
            vx_ref[:, g * LANES + HEAD_DIM:(g + 1) * LANES] = ones_col

    def body(j, carry):
        start = pl.multiple_of(j * tk, tk)
        out = []
        for g in range(N_KV_HEADS):
            m_prev, acc_prev = carry[g]
            sb = g % s_ref.shape[0]
            k_t = k_ref[pl.ds(start, tk), g * HEAD_DIM:(g + 1) * HEAD_DIM]
            s_ref[sb] = lax.dot_general(qs_ref[g], k_t, (((1,), (1,)), ((), ())),
                                        preferred_element_type=F32)
            m_new = jnp.maximum(m_prev, jnp.max(s_ref[sb], axis=-1, keepdims=True))
            alpha = jnp.exp2(m_prev - m_new)
            for rc in range(m_rows // rb):
                rows = slice(rc * rb, (rc + 1) * rb)
                p_ref[sb, rows, :] = jnp.exp2(s_ref[sb, rows, :] - m_new[rows]).astype(BF16)
            pv = jnp.dot(p_ref[sb], vx_ref[pl.ds(start, tk), g * LANES:(g + 1) * LANES],
                         preferred_element_type=F32)
            out.append((m_new, alpha * acc_prev + pv))
        return tuple(out)

    init = tuple((jnp.full((m_rows, 1), NEG_BIG, F32), jnp.zeros((m_rows, LANES), F32))
                 for _ in range(N_KV_HEADS))
    final = lax.fori_loop(0, n_kv, body, init, unroll=min(n_kv, 4))
    for g in range(N_KV_HEADS):
        acc = final[g][1]
        o = acc[:, :HEAD_DIM] / acc[:, HEAD_DIM:HEAD_DIM + 1]
        for r in range(group):
            hh = g * group + r
            o_ref[:, hh * HEAD_DIM:(hh + 1) * HEAD_DIM] = o[r * tq:(r + 1) * tq].astype(o_ref.dtype)


def _gqa_attention(proj, batch, seq, tq=256, tk=1024, rb=32):
    kern = functools.partial(_gqa_kernel, tk=tk, rb=rb)
    m_rows = (N_SELF_HEADS // N_KV_HEADS) * tq
    return pl.pallas_call(
        kern,
        grid=(batch, seq // tq),
        in_specs=[
            pl.BlockSpec((None, tq, SELF_W), lambda b, i: (b, i, 0)),
            pl.BlockSpec((None, seq, KV_W_A), lambda b, i: (b, 0, SELF_W // KV_W_A)),
            pl.BlockSpec((None, seq, KV_W_A), lambda b, i: (b, 0, SELF_W // KV_W_A + 1)),
        ],
        out_specs=pl.BlockSpec((None, tq, SELF_W), lambda b, i: (b, i, 0)),
        out_shape=jax.ShapeDtypeStruct((batch, seq, SELF_W), BF16),
        scratch_shapes=[pltpu.VMEM((N_KV_HEADS, m_rows, HEAD_DIM), BF16),
                        pltpu.VMEM((seq, N_KV_HEADS * LANES), BF16),
                        pltpu.VMEM((2, m_rows, tk), F32),
                        pltpu.VMEM((2, m_rows, tk), BF16)],
        compiler_params=_cparams(("parallel", "arbitrary")),
        name="gqa_attention",
    )(proj, proj, proj)


NA_Q_ROWS = 8
NA_K_ROWS = NA_Q_ROWS + ROW_WIN


def _natten_kernel(q_ref, k_ref, v_ref, tbl_ref, o_ref, bias_ref, s_ref, p_ref, *, rows, rb):
    i = pl.program_id(2)
    n_i = pl.num_programs(2)

    @pl.when((pl.program_id(1) == 0) & (i == 0))
    def _():
        neg = jnp.full((GRID_W, GRID_W), NEG_BIG, F32)
        for h in range(LANES // HEAD_DIM):
            for v, delta in enumerate((0, ROW_WIN // 2, ROW_WIN)):
                for qi in range(NA_Q_ROWS):
                    sr = min(max(qi + delta - ROW_WIN // 2, 0), NA_K_ROWS - ROW_WIN)
                    for w0 in range(0, NA_K_ROWS, LANES // GRID_W):
                        blks = []
                        for w in range(w0, w0 + LANES // GRID_W):
                            ok = sr <= w < sr + ROW_WIN
                            blks.append(tbl_ref[h, w - qi - delta + ROW_WIN - 1] if ok else neg)
                        bias_ref[h, v, qi * GRID_W:(qi + 1) * GRID_W,
                                 w0 * GRID_W:w0 * GRID_W + LANES] = jnp.concatenate(blks, axis=1)
    nq = NA_Q_ROWS * GRID_W
    nk = NA_K_ROWS * GRID_W
    n_blk = q_ref.shape[0] // nq
    n_tot = n_i * n_blk
    lane = lax.broadcasted_iota(jnp.int32, (nk, LANES - HEAD_DIM), 1)
    ones_col = jnp.where(lane == 0, 1.0, 0.0).astype(BF16)
    n_heads = LANES // HEAD_DIM
    hsl = [slice(h * HEAD_DIM, (h + 1) * HEAD_DIM) for h in range(n_heads)]
    chains = []
    for u in range(n_blk):
        blk = i * n_blk + u
        ws = jnp.clip(blk * NA_Q_ROWS - ROW_WIN // 2, 0, rows - NA_K_ROWS)
        start = pl.multiple_of(ws * GRID_W, GRID_W)
        variant = jnp.where(blk == 0, 0, jnp.where(blk == n_tot - 1, 2, 1))
        for h in range(n_heads):
            chains.append((u * n_heads + h, slice(u * nq, (u + 1) * nq), h, start, variant))
    for c, qrows, h, start, variant in chains:
        k_w = k_ref[pl.ds(start, nk), hsl[h]]
        s_ref[c] = (lax.dot_general(q_ref[qrows, hsl[h]], k_w, (((1,), (1,)), ((), ())),
                                    preferred_element_type=F32) + bias_ref[h, variant])
    for c, qrows, h, start, variant in chains:
        m = jnp.max(s_ref[c], axis=-1, keepdims=True)
        for rc in range(nq // rb):
            rs = slice(rc * rb, (rc + 1) * rb)
            p_ref[c, rs, :] = jnp.exp2(s_ref[c, rs, :] - m[rs]).astype(BF16)
    for c, qrows, h, start, variant in chains:
        vx = jnp.concatenate([v_ref[pl.ds(start, nk), hsl[h]], ones_col], axis=1)
        acc = jnp.dot(p_ref[c], vx, preferred_element_type=F32)
        o_ref[qrows, hsl[h]] = (acc[:, :HEAD_DIM] / acc[:, HEAD_DIM:HEAD_DIM + 1]).astype(o_ref.dtype)


def _natten_bias(rpb):
    cols = np.arange(GRID_W)
    col_start = np.clip(cols - COL_WIN // 2, 0, GRID_W - COL_WIN)
    cp = np.arange(GRID_W)
    col_ok = (cp[None, :] >= col_start[:, None]) & (cp[None, :] < col_start[:, None] + COL_WIN)
    dc = np.clip(cp[None, :] - cols[:, None] + (COL_WIN - 1), 0, 2 * COL_WIN - 2)
    t = rpb[:, :, dc] * LOG2E
    return jnp.where(jnp.asarray(col_ok)[None, None], t, NEG_BIG).astype(F32)


def _natten(proj, bias, batch, seq, rb=32, n_blk=2):
    rows = seq // GRID_W
    nq = NA_Q_ROWS * GRID_W
    nk = NA_K_ROWS * GRID_W
    hp = N_SELF_HEADS * HEAD_DIM // LANES
    hpb = LANES // HEAD_DIM
    assert rows % (n_blk * NA_Q_ROWS) == 0 and rows // NA_Q_ROWS >= 2 and rows >= NA_K_ROWS
    kern = functools.partial(_natten_kernel, rows=rows, rb=rb)
    return pl.pallas_call(
        kern,
        grid=(hp, batch, rows // (n_blk * NA_Q_ROWS)),
        in_specs=[
            pl.BlockSpec((None, n_blk * nq, LANES), lambda h, b, i: (b, i, h)),
            pl.BlockSpec((None, seq, LANES), lambda h, b, i: (b, 0, hp + h)),
            pl.BlockSpec((None, seq, LANES), lambda h, b, i: (b, 0, 2 * hp + h)),
            pl.BlockSpec((hpb, 2 * ROW_WIN - 1, GRID_W, GRID_W), lambda h, b, i: (h, 0, 0, 0)),
        ],
        out_specs=pl.BlockSpec((None, n_blk * nq, LANES), lambda h, b, i: (b, i, h)),
        out_shape=jax.ShapeDtypeStruct((batch, seq, SELF_W), BF16),
        scratch_shapes=[pltpu.VMEM((hpb, 3, nq, nk), F32),
                        pltpu.VMEM((n_blk * hpb, nq, nk), F32), pltpu.VMEM((n_blk * hpb, nq, nk), BF16)],
        compiler_params=_cparams(("parallel", "arbitrary", "arbitrary")),
        name="natten",
    )(proj, proj, proj, bias)


def _split_bf16(a):
    hi = a.astype(BF16)
    lo = (a - hi.astype(F32)).astype(BF16)
    return hi, lo


def _mix_kernel(x_ref, so_ref, qx_ref, kbd_ref, vbd_ref, obd_ref, wo_ref, g_ref, wrh_ref, wrl_ref,
                xo_ref, hn_ref, lg_ref):
    s = jnp.dot(qx_ref[...], kbd_ref[...], preferred_element_type=F32)
    ps = []
    for h in range(N_MEM_HEADS):
        s_h = s[:, h * N_MEM:(h + 1) * N_MEM]
        ps.append(jnp.exp2(s_h - jnp.max(s_h, axis=-1, keepdims=True)).astype(BF16))
    p = jnp.concatenate(ps, axis=-1)
    mem_out = (jnp.dot(p, vbd_ref[...], preferred_element_type=F32)
               / jnp.dot(p, obd_ref[...], preferred_element_type=F32)).astype(BF16)
    y = (x_ref[...]
         + jnp.dot(so_ref[...], wo_ref[:SELF_W, :], preferred_element_type=F32)
         + jnp.dot(mem_out, wo_ref[SELF_W:, :], preferred_element_type=F32))
    xo_ref[...] = y
    ms = jnp.mean(y * y, axis=-1, keepdims=True)
    hn = (y * lax.rsqrt(ms + EPS)) * g_ref[...]
    bits = pltpu.bitcast(hn.astype(BF16).astype(F32), jnp.uint32)
    half = hn.shape[1] // 2
    hn_ref[...] = (bits[:, :half] >> 16) | bits[:, half:]
    hi, lo = _split_bf16(hn)
    dn = (((1,), (1,)), ((), ()))
    lg = (lax.dot_general(wrh_ref[...], hi, dn, preferred_element_type=F32)
          + lax.dot_general(wrh_ref[...], lo, dn, preferred_element_type=F32)
          + lax.dot_general(wrl_ref[...], hi, dn, preferred_element_type=F32))
    for j in range(lg.shape[1] // LANES):
        lg_ref[:, j, :] = lg[:, j * LANES:(j + 1) * LANES]


def _mix(x2d, self_out2d, proj2d, qx_block, kv, wo_bf16, g_ffn, wr_t, seq, tm=1024):
    n, d = x2d.shape
    blocks_per_seq = seq // tm
    wrh, wrl = _split_bf16(wr_t)
    b = kv.shape[0]
    eye = jnp.eye(N_MEM_HEADS, dtype=kv.dtype)
    km = kv[..., :MEM_W].reshape(b, N_MEM, N_MEM_HEADS, HEAD_DIM)
    vm = kv[..., MEM_W:].reshape(b, N_MEM, N_MEM_HEADS, HEAD_DIM)
    kbd = (km.transpose(0, 2, 3, 1)[:, :, :, None, :] * eye[None, :, None, :, None]
           ).reshape(b, MEM_W, N_MEM_HEADS * N_MEM)
    vbd = (vm.transpose(0, 2, 1, 3)[:, :, :, None, :] * eye[None, :, None, :, None]
           ).reshape(b, N_MEM_HEADS * N_MEM, MEM_W)
    obd = jnp.asarray(np.kron(np.eye(N_MEM_HEADS), np.ones((N_MEM, HEAD_DIM))), BF16)
    return pl.pallas_call(
        _mix_kernel,
        grid=(n // tm,),
        in_specs=[
            pl.BlockSpec((tm, d), lambda i: (i, 0)),
            pl.BlockSpec((tm, SELF_W), lambda i: (i, 0)),
            pl.BlockSpec((tm, MEM_W), lambda i: (i, qx_block)),
            pl.BlockSpec((None, MEM_W, N_MEM_HEADS * N_MEM), lambda i: (i // blocks_per_seq, 0, 0)),
            pl.BlockSpec((None, N_MEM_HEADS * N_MEM, MEM_W), lambda i: (i // blocks_per_seq, 0, 0)),
            pl.BlockSpec((N_MEM_HEADS * N_MEM, MEM_W), lambda i: (0, 0)),
            pl.BlockSpec((d, d), lambda i: (0, 0)),
            pl.BlockSpec((1, d), lambda i: (0, 0)),
            pl.BlockSpec((N_EXPERTS, d), lambda i: (0, 0)),
            pl.BlockSpec((N_EXPERTS, d), lambda i: (0, 0)),
        ],
        out_specs=[
            pl.BlockSpec((tm, d), lambda i: (i, 0)),
            pl.BlockSpec((tm, d // 2), lambda i: (i, 0)),
            pl.BlockSpec((N_EXPERTS, tm // LANES, LANES), lambda i: (0, i, 0)),
        ],
        out_shape=[
            jax.ShapeDtypeStruct((n, d), F32),
            jax.ShapeDtypeStruct((n, d // 2), jnp.uint32),
            jax.ShapeDtypeStruct((N_EXPERTS, n // LANES, LANES), F32),
        ],
        compiler_params=_cparams(("parallel",)),
        name="mix_out_router",
    )(x2d, self_out2d, proj2d, kbd, vbd, obd, wo_bf16, g_ffn.reshape(1, d), wrh, wrl)


def _route_kernel(lg_ref, idx_ref, gate_ref, pos_ref, offs_ref, aff_s, gt_s, eq_s, need_s, *, cap):
    lg = lg_ref[...]
    n_e, nbk, _ = lg.shape
    ex = jnp.exp(lg - jnp.max(lg, axis=0, keepdims=True))
    aff = ex / jnp.sum(ex, axis=0, keepdims=True)
    keys = pltpu.bitcast(aff, jnp.int32)

    def count(mask_f):
        return jnp.sum(jnp.sum(mask_f, axis=1, keepdims=True), axis=2, keepdims=True)

    def bisect(it, thr):
        cand = thr | jnp.left_shift(jnp.int32(1), 30 - it)
        cnt = count(jnp.where(keys >= cand, 1.0, 0.0))
        return jnp.where(cnt >= float(cap), cand, thr)

    thr = lax.fori_loop(0, 31, bisect, jnp.zeros((n_e, 1, 1), jnp.int32))
    gt = jnp.where(keys > thr, 1.0, 0.0)
    eq = jnp.where(keys == thr, 1.0, 0.0)
    need = float(cap) - count(gt)
    aff_s[...] = aff
    gt_s[...] = gt
    eq_s[...] = eq
    need_s[...] = jnp.broadcast_to(need, need_s.shape)

    def tri(shape, fn):
        r = lax.broadcasted_iota(jnp.int32, shape, 0)
        c = lax.broadcasted_iota(jnp.int32, shape, 1)
        return jnp.where(fn(r, c), 1.0, 0.0).astype(BF16)

    u_lane = tri((LANES, LANES), lambda r, c: r <= c)
    l_blk = tri((nbk, nbk), lambda r, c: c < r)
    u_blk = tri((nbk, nbk), lambda r, c: r <= c)
    ones_l = jnp.ones((LANES, LANES), BF16)
    ones_b = jnp.ones((nbk, LANES), BF16)
    ones_8 = jnp.ones((8, LANES), BF16)
    dn_t = (((1,), (1,)), ((), ()))

    def cums(m):
        loc = jnp.dot(m.astype(BF16), u_lane, preferred_element_type=F32)
        tot = jnp.broadcast_to(loc[:, LANES - 1:LANES], (nbk, LANES))
        offs = jnp.dot(l_blk, tot.astype(BF16), preferred_element_type=F32)
        return loc, offs

    p_b = lax.broadcasted_iota(jnp.int32, (cap, nbk), 0).astype(F32)
    p_l = lax.broadcasted_iota(jnp.int32, (cap, LANES), 0).astype(F32)
    lane_b = lax.broadcasted_iota(jnp.int32, (cap, nbk), 1).astype(F32)
    lane_l = lax.broadcasted_iota(jnp.int32, (cap, LANES), 1).astype(F32)
    eye = (lax.broadcasted_iota(jnp.int32, (LANES, LANES), 0)
           == lax.broadcasted_iota(jnp.int32, (LANES, LANES), 1))

    def per_expert(e, _):
        eqm = eq_s[e]
        loc, offs = cums(eqm)
        tie_rank = loc + offs - eqm
        sel = gt_s[e] + eqm * jnp.where(tie_rank < need_s[e][0:1, 0:1], 1.0, 0.0)
        loc2, offs2 = cums(sel)
        pos_ref[e] = jnp.where(sel > 0.0, loc2 + offs2 - sel, -1.0).astype(jnp.int32)
        offs_ref[e] = offs2.astype(jnp.int32)
        tot_row = lax.dot_general(ones_8, sel.astype(BF16), dn_t, preferred_element_type=F32)
        s_row = jnp.dot(tot_row.astype(BF16), u_blk, preferred_element_type=F32)[0:1, :]
        nbv = jnp.dot(jnp.where(s_row <= p_b, 1.0, 0.0).astype(BF16), ones_b, preferred_element_type=F32)
        onehot = jnp.where(lane_b == nbv[:, :nbk], 1.0, 0.0).astype(BF16)
        g_loc = jnp.dot(onehot, loc2.astype(BF16), preferred_element_type=F32)
        o_hi = jnp.floor(offs2 * (1.0 / LANES))
        o_lo = offs2 - o_hi * LANES
        offp = (jnp.dot(onehot, o_hi.astype(BF16), preferred_element_type=F32) * LANES
                + jnp.dot(onehot, o_lo.astype(BF16), preferred_element_type=F32))
        il = jnp.dot(jnp.where(g_loc <= p_l - offp, 1.0, 0.0).astype(BF16), ones_l,
                     preferred_element_type=F32)
        tok = nbv * LANES + il
        a = aff_s[e]
        a1 = a.astype(BF16)
        r1 = a - a1.astype(F32)
        a2 = r1.astype(BF16)
        a3 = (r1 - a2.astype(F32)).astype(BF16)
        g_aff = (jnp.dot(onehot, a1, preferred_element_type=F32)
                 + jnp.dot(onehot, a2, preferred_element_type=F32)
                 + jnp.dot(onehot, a3, preferred_element_type=F32))
        gate_ref[e] = jnp.sum(jnp.where(lane_l == il, g_aff, 0.0), axis=1, keepdims=True)
        tok3 = tok.reshape(cap // LANES, LANES, LANES)
        idx_ref[e] = jnp.sum(jnp.where(eye[None], tok3, 0.0), axis=1).astype(jnp.int32)
        return 0

    lax.fori_loop(0, n_e, per_expert, 0)


def _route(logits_blk, cap):
    n_e, nbk, _ = logits_blk.shape
    return pl.pallas_call(
        functools.partial(_route_kernel, cap=cap),
        out_shape=[
            jax.ShapeDtypeStruct((n_e, cap // LANES, LANES), jnp.int32),
            jax.ShapeDtypeStruct((n_e, cap, 1), F32),
            jax.ShapeDtypeStruct((n_e, nbk, LANES), jnp.int32),
            jax.ShapeDtypeStruct((n_e, nbk, LANES), jnp.int32),
        ],
        scratch_shapes=[pltpu.VMEM((n_e, nbk, LANES), F32), pltpu.VMEM((n_e, nbk, LANES), F32),
                        pltpu.VMEM((n_e, nbk, LANES), F32), pltpu.VMEM((n_e, 8, LANES), F32)],
        compiler_params=pltpu.CompilerParams(vmem_limit_bytes=VMEM_LIMIT),
        name="route",
    )(logits_blk)


def _ffn_kernel(idx_ref, hn_hbm, gate_ref, wg_ref, wu_ref, wd_ref, o_ref, buf, sem, wgb, wub, wdb, *,
                chunk, cap, n_f):
    e = pl.program_id(0)
    f = pl.program_id(1)
    n_e = pl.num_programs(0)
    slot = e % 2
    half = wgb.shape[0] // 2

    def issue(expert, slot_, r0, count):
        def body(r, _):
            row = r0 + r
            tok = idx_ref[expert * cap + row]
            pltpu.make_async_copy(hn_hbm.at[pl.ds(tok, 1), :], buf.at[slot_, pl.ds(row, 1), :],
                                  sem.at[slot_]).start()
            return 0
        lax.fori_loop(0, count, body, 0, unroll=8)

    @pl.when((e == 0) & (f == 0))
    def _():
        issue(0, 0, 0, cap)

    @pl.when(f == 0)
    def _():
        pltpu.make_async_copy(buf.at[slot], buf.at[slot], sem.at[slot]).wait()
        o_ref[...] = jnp.zeros(o_ref.shape, F32)

    wgb[...] = wg_ref[...].astype(BF16)

    nxt = jnp.minimum(e + 1, n_e - 1)
    n_chunks = cap // chunk
    per_step = cap // n_f
    quota = -(-per_step // max(n_chunks - 1, 1))
    issue_at = [min(c * quota, per_step) for c in range(n_chunks)] + [per_step]
    if n_chunks > 1:
        issue_at[n_chunks - 1] = per_step
    last = f == n_f - 1

    for c in range(n_chunks):
        rs = slice(c * chunk, (c + 1) * chunk)
        w = buf[slot, rs, :]
        x_lo = pltpu.bitcast(w << 16, F32).astype(BF16)
        x_hi = pltpu.bitcast(w & jnp.uint32(0xFFFF0000), F32).astype(BF16)
        gg = (jnp.dot(x_lo, wgb[:half, :], preferred_element_type=F32)
              + jnp.dot(x_hi, wgb[half:, :], preferred_element_type=F32))
        for r in range(issue_at[c], issue_at[c + 1]):
            row = f * per_step + r
            tok = idx_ref[nxt * cap + row]
            pltpu.make_async_copy(hn_hbm.at[pl.ds(tok, 1), :], buf.at[1 - slot, pl.ds(row, 1), :],
                                  sem.at[1 - slot]).start()
        if c == 0:
            wub[...] = wu_ref[...].astype(BF16)
        uu = (jnp.dot(x_lo, wub[:half, :], preferred_element_type=F32)
              + jnp.dot(x_hi, wub[half:, :], preferred_element_type=F32))
        if c == 0:
            wdb[...] = wd_ref[...].astype(BF16)
        hid = (jax.nn.silu(gg) * uu).astype(BF16)
        part = jnp.dot(hid, wdb[...], preferred_element_type=F32)
        o_ref[rs, :] = (o_ref[rs, :] + part) * jnp.where(last, gate_ref[rs, :], 1.0)

    @pl.when((e == n_e - 1) & last)
    def _():
        pltpu.make_async_copy(buf.at[1 - slot], buf.at[1 - slot], sem.at[1 - slot]).wait()


def _expert_ffn(idx_flat, hn_packed, gates_col, w_gate, w_up, w_down, layer, cap, tf=512, chunk=512):
    d = 2 * hn_packed.shape[1]
    n_f = D_EXPERT // tf
    assert cap % (n_f * (cap // chunk)) == 0 and cap % chunk == 0
    kern = functools.partial(_ffn_kernel, chunk=chunk, cap=cap, n_f=n_f)
    grid_spec = pltpu.PrefetchScalarGridSpec(
        num_scalar_prefetch=1,
        grid=(N_EXPERTS, n_f),
        in_specs=[
            pl.BlockSpec(memory_space=pl.ANY),
            pl.BlockSpec((cap, 1), lambda e, f, idx: (e, 0)),
            pl.BlockSpec((None, None, d, tf), lambda e, f, idx: (layer, e, 0, f)),
            pl.BlockSpec((None, None, d, tf), lambda e, f, idx: (layer, e, 0, f)),
            pl.BlockSpec((None, None, tf, d), lambda e, f, idx: (layer, e, f, 0)),
        ],
        out_specs=pl.BlockSpec((cap, d), lambda e, f, idx: (e, 0)),
        scratch_shapes=[pltpu.VMEM((2, cap, d // 2), jnp.uint32), pltpu.SemaphoreType.DMA((2,)),
                        pltpu.VMEM((d, tf), BF16), pltpu.VMEM((d, tf), BF16), pltpu.VMEM((tf, d), BF16)],
    )
    return pl.pallas_call(
        kern,
        grid_spec=grid_spec,
        out_shape=jax.ShapeDtypeStruct((N_EXPERTS * cap, d), F32),
        compiler_params=_cparams(("arbitrary", "arbitrary")),
        name="expert_ffn",
    )(idx_flat, hn_packed, gates_col, w_gate, w_up, w_down)


SEG = 48
SEG_AL = SEG + 8


def _combine_kernel(offs_ref, y_ref, pos_ref, jl_ref, rmat_ref, ye_hbm, o_ref, stag, sem, *,
                    cap, nbk, blocks_per_tile):
    i = pl.program_id(0)
    n_t = pl.num_programs(0)
    slot = i % 2
    tile_rows = y_ref.shape[0]

    def seg(tile, e, k):
        base = offs_ref[e * (nbk + 1) + tile * blocks_per_tile] + k * SEG
        src = jnp.minimum((base // 8) * 8, cap - SEG_AL)
        return base, src

    def issue(tile, k, slot_):
        for e in range(N_EXPERTS):
            _, src = seg(tile, e, k)
            pltpu.make_async_copy(ye_hbm.at[pl.ds(pl.multiple_of(e * cap + src, 8), SEG_AL), :],
                                  stag.at[slot_, pl.ds(e * SEG_AL, SEG_AL), :], sem.at[slot_]).start()

    def wait(slot_):
        pltpu.make_async_copy(stag.at[slot_], stag.at[slot_], sem.at[slot_]).wait()

    def contrib(k):
        pos = pos_ref[...]
        lane_e = lax.broadcasted_iota(jnp.int32, pos.shape, 1)
        base_v = jnp.zeros(pos.shape, jnp.int32)
        src_v = jnp.zeros(pos.shape, jnp.int32)
        for e in range(N_EXPERTS):
            base, src = seg(i, e, k)
            base_v = jnp.where(lane_e == e, base, base_v)
            src_v = jnp.where(lane_e == e, src, src_v)
        valid = (pos >= base_v) & (pos < base_v + SEG)
        rel = jnp.where(valid, pos - src_v, -1).astype(F32).astype(BF16)
        rep = jnp.dot(rel, rmat_ref[...], preferred_element_type=F32)
        onehot = jnp.where(rep == jl_ref[...], 1.0, 0.0).astype(BF16)
        st = stag[slot]
        hi = st.astype(BF16)
        lo = (st - hi.astype(F32)).astype(BF16)
        return (jnp.dot(onehot, hi, preferred_element_type=F32)
                + jnp.dot(onehot, lo, preferred_element_type=F32))

    @pl.when(i == 0)
    def _():
        issue(0, 0, 0)

    @pl.when(i + 1 < n_t)
    def _():
        issue(i + 1, 0, 1 - slot)

    wait(slot)
    acc = y_ref[...] + contrib(0)

    max_cnt = jnp.int32(0)
    for e in range(N_EXPERTS):
        o0 = offs_ref[e * (nbk + 1) + i * blocks_per_tile]
        o1 = offs_ref[e * (nbk + 1) + (i + 1) * blocks_per_tile]
        max_cnt = jnp.maximum(max_cnt, o1 - o0)
    n_rounds = (max_cnt + SEG - 1) // SEG

    def extra(k, acc_):
        issue(i, k, slot)
        wait(slot)
        return acc_ + contrib(k)

    o_ref[...] = lax.fori_loop(1, n_rounds, extra, acc)


def _combine(offs_flat, y2d, pos_t, ye, cap, nbk, tile=256):
    n, d = y2d.shape
    ktot = N_EXPERTS * SEG_AL
    lane = np.arange(ktot)
    jl = jnp.asarray((lane % SEG_AL)[None, :], F32)
    rmat = jnp.asarray((lane[None, :] // SEG_AL) == np.arange(N_EXPERTS)[:, None], BF16)
    kern = functools.partial(_combine_kernel, cap=cap, nbk=nbk, blocks_per_tile=tile // LANES)
    grid_spec = pltpu.PrefetchScalarGridSpec(
        num_scalar_prefetch=1,
        grid=(n // tile,),
        in_specs=[
            pl.BlockSpec((tile, d), lambda i, offs: (i, 0)),
            pl.BlockSpec((tile, N_EXPERTS), lambda i, offs: (i, 0)),
            pl.BlockSpec((1, ktot), lambda i, offs: (0, 0)),
            pl.BlockSpec((N_EXPERTS, ktot), lambda i, offs: (0, 0)),
            pl.BlockSpec(memory_space=pl.ANY),
        ],
        out_specs=pl.BlockSpec((tile, d), lambda i, offs: (i, 0)),
        scratch_shapes=[pltpu.VMEM((2, ktot, d), F32), pltpu.SemaphoreType.DMA((2,))],
    )
    return pl.pallas_call(
        kern,
        grid_spec=grid_spec,
        out_shape=jax.ShapeDtypeStruct((n, d), F32),
        compiler_params=_cparams(("arbitrary",)),
        name="combine",
    )(offs_flat, y2d, pos_t, jl, rmat, ye)


def _tile_gain(g64, n_heads, scale=1.0):
    return jnp.tile(g64.astype(F32) * scale, n_heads)


def _prepare(p):
    q = dict(p)
    for name in ("w_in_a", "w_in_b", "w_mem_kv", "w_o"):
        q[name] = p[name].astype(BF16)
    q["na_bias"] = [_natten_bias(p["na_rpb"][j]) for j in range(p["na_rpb"].shape[0])]
    q["prepared"] = True
    return q


def _trunk(x, mem, p):
    if "prepared" not in p:
        p = _prepare(p)
    batch, seq, d = x.shape
    n = batch * seq
    cap = EC_CAPACITY_FACTOR * n // N_EXPERTS
    cos_t, sin_t = _rope_tables(seq)
    cos_m, sin_m = cos_t[:N_MEM], sin_t[:N_MEM]
    qscale = HEAD_DIM ** -0.5 * LOG2E
    x2d = x.reshape(n, d)
    mem2d = mem.reshape(batch * N_MEM, d)
    ones_kv = jnp.ones((MEM_W,), F32)
    for i in range(DEPTH):
        j = i // 2
        xq_gain = _tile_gain(p["xq_norm"][i], N_MEM_HEADS, qscale)
        if i % 2 == 0:
            gain = jnp.concatenate([_tile_gain(p["q_norm_a"][j], N_SELF_HEADS, qscale),
                                    _tile_gain(p["k_norm_a"][j], N_KV_HEADS),
                                    jnp.ones((KV_W_A,), F32), xq_gain])
            mask = [True] * 8 + [False] * 2 + [True] * 2
            proj = _fused_proj(x2d, p["norm_mix"][i], p["w_in_a"][j], gain, mask, 8,
                               cos_t, sin_t, seq, tm=512)
            self_out = _gqa_attention(proj.reshape(batch, seq, IN_W_A), batch, seq)
            qx_block = (SELF_W + 2 * KV_W_A) // MEM_W
        else:
            gain = jnp.concatenate([_tile_gain(p["q_norm_b"][j], N_SELF_HEADS, qscale),
                                    _tile_gain(p["k_norm_b"][j], N_SELF_HEADS),
                                    jnp.ones((SELF_W,), F32), xq_gain])
            mask = [True] * 12 + [False] * 6 + [True] * 2
            proj = _fused_proj(x2d, p["norm_mix"][i], p["w_in_b"][j], gain, mask, 0,
                               cos_t, sin_t, seq, tm=512)
            self_out = _natten(proj.reshape(batch, seq, IN_W_B), p["na_bias"][j], batch, seq)
            qx_block = 3 * SELF_W // MEM_W
        kv_gain = jnp.concatenate([_tile_gain(p["xk_norm"][i], N_MEM_HEADS), ones_kv])
        kv = _fused_proj(mem2d, p["norm_mem"][i], p["w_mem_kv"][i], kv_gain,
                         [True, True, False, False], 0, cos_m, sin_m, N_MEM)
        x2d, hn, logits_t = _mix(x2d, self_out.reshape(n, SELF_W), proj, qx_block,
                                 kv.reshape(batch, N_MEM, 2 * MEM_W), p["w_o"][i],
                                 p["norm_ffn"][i], p["w_router"][i].T, seq)
        nbk = n // LANES
        idx, gates, pos, offs = _route(logits_t, cap)
        ye = _expert_ffn(idx.reshape(-1), hn, gates.reshape(-1, 1),
                         p["w_gate"], p["w_up"], p["w_down"], i, cap)
        offs_flat = jnp.concatenate([offs[:, :, 0], jnp.full((N_EXPERTS, 1), cap, jnp.int32)],
                                    axis=1).reshape(-1)
        x2d = _combine(offs_flat, x2d, pos.reshape(N_EXPERTS, n).T, ye, cap, nbk)
    return x2d.reshape(batch, seq, d)


def kernel(x_prompt, x_sample, mem_prompt, mem_sample, norm_mix, w_in_a, q_norm_a, k_norm_a, w_in_b, q_norm_b, k_norm_b, na_rpb, norm_mem, w_mem_kv, xq_norm, xk_norm, w_o, norm_ffn, w_router, w_gate, w_up, w_down):
    p = dict(norm_mix=norm_mix, w_in_a=w_in_a, q_norm_a=q_norm_a, k_norm_a=k_norm_a, w_in_b=w_in_b,
             q_norm_b=q_norm_b, k_norm_b=k_norm_b, na_rpb=na_rpb, norm_mem=norm_mem, w_mem_kv=w_mem_kv,
             xq_norm=xq_norm, xk_norm=xk_norm, w_o=w_o, norm_ffn=norm_ffn, w_router=w_router,
             w_gate=w_gate, w_up=w_up, w_down=w_down)
    p = _prepare(p)
    return _trunk(x_prompt, mem_prompt, p), _trunk(x_sample, mem_sample, p)
```

```python
import functools
import math

import jax
import jax.numpy as jnp
import numpy as np
from jax import lax
from jax.experimental import pallas as pl
from jax.experimental.pallas import tpu as pltpu

F32 = jnp.float32
BF16 = jnp.bfloat16

D_MODEL = 1024
DEPTH = 4
GRID_W = 64
HEAD_DIM = 64
N_SELF_HEADS = 12
N_KV_HEADS = 4
N_MEM_HEADS = 4
N_MEM = 256
SELF_W = N_SELF_HEADS * HEAD_DIM
KV_W_A = N_KV_HEADS * HEAD_DIM
MEM_W = N_MEM_HEADS * HEAD_DIM
IN_W_A = SELF_W + 2 * KV_W_A + MEM_W
IN_W_B = 3 * SELF_W + MEM_W
ROW_WIN = 8
COL_WIN = 16
ROPE_THETA = 10000.0
ROPE_AXIS_DIM = HEAD_DIM // 2
N_EXPERTS = 16
EC_CAPACITY_FACTOR = 2
D_EXPERT = 2 * D_MODEL
EPS = 1e-6

LANES = 128
SLAB = 256
VMEM_LIMIT = 56 * 1024 * 1024
NEG_BIG = -1e30
LOG2E = math.log2(math.e)


def _cparams(sem):
    return pltpu.CompilerParams(dimension_semantics=sem, vmem_limit_bytes=VMEM_LIMIT)


def _swap16(y):
    lane = lax.broadcasted_iota(jnp.int32, y.shape, 1)
    up = pltpu.roll(y, 16, axis=1)
    dn = pltpu.roll(y, y.shape[1] - 16, axis=1)
    return jnp.where((lane & 16) != 0, up, dn)


def _proj_kernel(x_ref, g_ref, w_ref, gain_ref, cos_ref, sin_ref, bd_ref, o_ref, *,
                 n_norm_slabs_mask, n_rope_slabs):
    x = x_ref[...]
    ms = jnp.mean(x * x, axis=-1, keepdims=True)
    h = (x * lax.rsqrt(ms + EPS)) * g_ref[...]
    proj = jnp.dot(h.astype(BF16), w_ref[...], preferred_element_type=F32)
    n_slabs = proj.shape[1] // SLAB
    bd = bd_ref[...]
    for s in range(n_slabs):
        cols = slice(s * SLAB, (s + 1) * SLAB)
        y = proj[:, cols]
        if n_norm_slabs_mask[s]:
            sq = y * y
            hi = sq.astype(BF16)
            lo = (sq - hi.astype(F32)).astype(BF16)
            ss = (jnp.dot(hi, bd, preferred_element_type=F32)
                  + jnp.dot(lo, bd, preferred_element_type=F32))
            y = (y * lax.rsqrt(ss * (1.0 / HEAD_DIM) + EPS)) * gain_ref[:, cols]
        if s < n_rope_slabs:
            y = y * cos_ref[...] + _swap16(y) * sin_ref[...]
        o_ref[:, cols] = y.astype(o_ref.dtype)


def _fused_proj(x2d, g, w_bf16, gain_full, norm_mask, n_rope_slabs, cos_t, sin_t, seq, tm=256):
    n, d = x2d.shape
    w_out = w_bf16.shape[1]
    assert n % tm == 0 and seq % tm == 0 and w_out % SLAB == 0
    blocks_per_seq = seq // tm
    per = SLAB // LANES
    assert all(len(set(norm_mask[i:i + per])) == 1 for i in range(0, len(norm_mask), per))
    assert n_rope_slabs % per == 0
    bd = jnp.asarray(np.kron(np.eye(SLAB // HEAD_DIM), np.ones((HEAD_DIM, HEAD_DIM))), BF16)
    kern = functools.partial(_proj_kernel, n_norm_slabs_mask=tuple(norm_mask[::per]),
                             n_rope_slabs=n_rope_slabs // per)
    return pl.pallas_call(
        kern,
        grid=(n // tm,),
        in_specs=[
            pl.BlockSpec((tm, d), lambda i: (i, 0)),
            pl.BlockSpec((1, d), lambda i: (0, 0)),
            pl.BlockSpec((d, w_out), lambda i: (0, 0)),
            pl.BlockSpec((1, w_out), lambda i: (0, 0)),
            pl.BlockSpec((tm, SLAB), lambda i: (i % blocks_per_seq, 0)),
            pl.BlockSpec((tm, SLAB), lambda i: (i % blocks_per_seq, 0)),
            pl.BlockSpec((SLAB, SLAB), lambda i: (0, 0)),
        ],
        out_specs=pl.BlockSpec((tm, w_out), lambda i: (i, 0)),
        out_shape=jax.ShapeDtypeStruct((n, w_out), BF16),
        compiler_params=_cparams(("parallel",)),
        name="fused_proj",
    )(x2d, g.reshape(1, d), w_bf16, gain_full.reshape(1, w_out), cos_t, sin_t, bd)


def _rope_tables(seq):
    t = jnp.arange(seq)
    row = (t // GRID_W).astype(F32)
    col = (t % GRID_W).astype(F32)
    inv = 1.0 / (ROPE_THETA ** (jnp.arange(0, ROPE_AXIS_DIM, 2, dtype=F32) / ROPE_AXIS_DIM))
    ar = row[:, None] * inv[None, :]
    ac = col[:, None] * inv[None, :]
    cr, sr, cc, sc = jnp.cos(ar), jnp.sin(ar), jnp.cos(ac), jnp.sin(ac)
    cos_h = jnp.concatenate([cr, cr, cc, cc], axis=-1)
    sin_h = jnp.concatenate([-sr, sr, -sc, sc], axis=-1)
    reps = SLAB // HEAD_DIM
    return jnp.tile(cos_h, (1, reps)), jnp.tile(sin_h, (1, reps))


def _gqa_kernel(q_ref, k_ref, v_ref, o_ref, qs_ref, vx_ref, s_ref, p_ref, *, tk, rb):
    tq = q_ref.shape[0]
    seq = k_ref.shape[0]
    group = N_SELF_HEADS // N_KV_HEADS
    n_kv = seq // tk
    for g in range(N_KV_HEADS):
        for r in range(group):
            hh = g * group + r
            qs_ref[g, r * tq:(r + 1) * tq, :] = q_ref[:, hh * HEAD_DIM:(hh + 1) * HEAD_DIM]
    m_rows = group * tq

    @pl.when(pl.program_id(1) == 0)
    def _():
        lane = lax.broadcasted_iota(jnp.int32, (seq, LANES - HEAD_DIM), 1)
        ones_col = jnp.where(lane == 0, 1.0, 0.0).astype(BF16)
        for g in range(N_KV_HEADS):
            vx_ref[:, g * LANES:g * LANES + HEAD_DIM] = v_ref[:, g * HEAD_DIM:(g + 1) * HEAD_DIM]
            vx_ref[:, g * LANES + HEAD_DIM:(g + 1) * LANES] = ones_col

    def body(j, carry):
        start = pl.multiple_of(j * tk, tk)
        out = []
        for g in range(N_KV_HEADS):
            m_prev, acc_prev = carry[g]
            sb = g % s_ref.shape[0]
            k_t = k_ref[pl.ds(start, tk), g * HEAD_DIM:(g + 1) * HEAD_DIM]
            s_ref[sb] = lax.dot_general(qs_ref[g], k_t, (((1,), (1,)), ((), ())),
                                        preferred_element_type=F32)
            m_new = jnp.maximum(m_prev, jnp.max(s_ref[sb], axis=-1, keepdims=True))
            alpha = jnp.exp2(m_prev - m_new)
            for rc in range(m_rows // rb):
                rows = slice(rc * rb, (rc + 1) * rb)
                p_ref[sb, rows, :] = jnp.exp2(s_ref[sb, rows, :] - m_new[rows]).astype(BF16)
            pv = jnp.dot(p_ref[sb], vx_ref[pl.ds(start, tk), g * LANES:(g + 1) * LANES],
                         preferred_element_type=F32)
            out.append((m_new, alpha * acc_prev + pv))
        return tuple(out)

    init = tuple((jnp.full((m_rows, 1), NEG_BIG, F32), jnp.zeros((m_rows, LANES), F32))
                 for _ in range(N_KV_HEADS))
    final = lax.fori_loop(0, n_kv, body, init, unroll=min(n_kv, 4))
    for g in range(N_KV_HEADS):
        acc = final[g][1]
        o = acc[:, :HEAD_DIM] / acc[:, HEAD_DIM:HEAD_DIM + 1]
        for r in range(group):
            hh = g * group + r
            o_ref[:, hh * HEAD_DIM:(hh + 1) * HEAD_DIM] = o[r * tq:(r + 1) * tq].astype(o_ref.dtype)


def _gqa_attention(proj, batch, seq, tq=256, tk=1024, rb=32):
    kern = functools.partial(_gqa_kernel, tk=tk, rb=rb)
    m_rows = (N_SELF_HEADS // N_KV_HEADS) * tq
    return pl.pallas_call(
        kern,
        grid=(batch, seq // tq),
        in_specs=[
            pl.BlockSpec((None, tq, SELF_W), lambda b, i: (b, i, 0)),
            pl.BlockSpec((None, seq, KV_W_A), lambda b, i: (b, 0, SELF_W // KV_W_A)),
            pl.BlockSpec((None, seq, KV_W_A), lambda b, i: (b, 0, SELF_W // KV_W_A + 1)),
        ],
        out_specs=pl.BlockSpec((None, tq, SELF_W), lambda b, i: (b, i, 0)),
        out_shape=jax.ShapeDtypeStruct((batch, seq, SELF_W), BF16),
        scratch_shapes=[pltpu.VMEM((N_KV_HEADS, m_rows, HEAD_DIM), BF16),
                        pltpu.VMEM((seq, N_KV_HEADS * LANES), BF16),
                        pltpu.VMEM((2, m_rows, tk), F32),
                        pltpu.VMEM((2, m_rows, tk), BF16)],
        compiler_params=_cparams(("parallel", "arbitrary")),
        name="gqa_attention",
    )(proj, proj, proj)


NA_Q_ROWS = 8
NA_K_ROWS = NA_Q_ROWS + ROW_WIN


def _natten_kernel(q_ref, k_ref, v_ref, tbl_ref, o_ref, bias_ref, s_ref, p_ref, *, rows, rb):
    i = pl.program_id(2)
    n_i = pl.num_programs(2)

    @pl.when((pl.program_id(1) == 0) & (i == 0))
    def _():
        neg = jnp.full((GRID_W, GRID_W), NEG_BIG, F32)
        for h in range(LANES // HEAD_DIM):
            for v, delta in enumerate((0, ROW_WIN // 2, ROW_WIN)):
                for qi in range(NA_Q_ROWS):
                    sr = min(max(qi + delta - ROW_WIN // 2, 0), NA_K_ROWS - ROW_WIN)
                    for w0 in range(0, NA_K_ROWS, LANES // GRID_W):
                        blks = []
                        for w in range(w0, w0 + LANES // GRID_W):
                            ok = sr <= w < sr + ROW_WIN
                            blks.append(tbl_ref[h, w - qi - delta + ROW_WIN - 1] if ok else neg)
                        bias_ref[h, v, qi * GRID_W:(qi + 1) * GRID_W,
                                 w0 * GRID_W:w0 * GRID_W + LANES] = jnp.concatenate(blks, axis=1)
    nq = NA_Q_ROWS * GRID_W
    nk = NA_K_ROWS * GRID_W
    n_blk = q_ref.shape[0] // nq
    n_tot = n_i * n_blk
    lane = lax.broadcasted_iota(jnp.int32, (nk, LANES - HEAD_DIM), 1)
    ones_col = jnp.where(lane == 0, 1.0, 0.0).astype(BF16)
    n_heads = LANES // HEAD_DIM
    hsl = [slice(h * HEAD_DIM, (h + 1) * HEAD_DIM) for h in range(n_heads)]
    chains = []
    for u in range(n_blk):
        blk = i * n_blk + u
        ws = jnp.clip(blk * NA_Q_ROWS - ROW_WIN // 2, 0, rows - NA_K_ROWS)
        start = pl.multiple_of(ws * GRID_W, GRID_W)
        variant = jnp.where(blk == 0, 0, jnp.where(blk == n_tot - 1, 2, 1))
        for h in range(n_heads):
            chains.append((u * n_heads + h, slice(u * nq, (u + 1) * nq), h, start, variant))
    for c, qrows, h, start, variant in chains:
        k_w = k_ref[pl.ds(start, nk), hsl[h]]
        s_ref[c] = (lax.dot_general(q_ref[qrows, hsl[h]], k_w, (((1,), (1,)), ((), ())),
                                    preferred_element_type=F32) + bias_ref[h, variant])
    for c, qrows, h, start, variant in chains:
        m = jnp.max(s_ref[c], axis=-1, keepdims=True)
        for rc in range(nq // rb):
            rs = slice(rc * rb, (rc + 1) * rb)
            p_ref[c, rs, :] = jnp.exp2(s_ref[c, rs, :] - m[rs]).astype(BF16)
    for c, qrows, h, start, variant in chains:
        vx = jnp.concatenate([v_ref[pl.ds(start, nk), hsl[h]], ones_col], axis=1)
        acc = jnp.dot(p_ref[c], vx, preferred_element_type=F32)
        o_ref[qrows, hsl[h]] = (acc[:, :HEAD_DIM] / acc[:, HEAD_DIM:HEAD_DIM + 1]).astype(o_ref.dtype)


def _natten_bias(rpb):
    cols = np.arange(GRID_W)
    col_start = np.clip(cols - COL_WIN // 2, 0, GRID_W - COL_WIN)
    cp = np.arange(GRID_W)
    col_ok = (cp[None, :] >= col_start[:, None]) & (cp[None, :] < col_start[:, None] + COL_WIN)
    dc = np.clip(cp[None, :] - cols[:, None] + (COL_WIN - 1), 0, 2 * COL_WIN - 2)
    t = rpb[:, :, dc] * LOG2E
    return jnp.where(jnp.asarray(col_ok)[None, None], t, NEG_BIG).astype(F32)


def _natten(proj, bias, batch, seq, rb=32, n_blk=4):
    rows = seq // GRID_W
    nq = NA_Q_ROWS * GRID_W
    nk = NA_K_ROWS * GRID_W
    hp = N_SELF_HEADS * HEAD_DIM // LANES
    hpb = LANES // HEAD_DIM
    assert rows % (n_blk * NA_Q_ROWS) == 0 and rows // NA_Q_ROWS >= 2 and rows >= NA_K_ROWS
    kern = functools.partial(_natten_kernel, rows=rows, rb=rb)
    return pl.pallas_call(
        kern,
        grid=(hp, batch, rows // (n_blk * NA_Q_ROWS)),
        in_specs=[
            pl.BlockSpec((None, n_blk * nq, LANES), lambda h, b, i: (b, i, h)),
            pl.BlockSpec((None, seq, LANES), lambda h, b, i: (b, 0, hp + h)),
            pl.BlockSpec((None, seq, LANES), lambda h, b, i: (b, 0, 2 * hp + h)),
            pl.BlockSpec((hpb, 2 * ROW_WIN - 1, GRID_W, GRID_W), lambda h, b, i: (h, 0, 0, 0)),
        ],
        out_specs=pl.BlockSpec((None, n_blk * nq, LANES), lambda h, b, i: (b, i, h)),
        out_shape=jax.ShapeDtypeStruct((batch, seq, SELF_W), BF16),
        scratch_shapes=[pltpu.VMEM((hpb, 3, nq, nk), F32),
                        pltpu.VMEM((n_blk * hpb, nq, nk), F32), pltpu.VMEM((n_blk * hpb, nq, nk), BF16)],
        compiler_params=_cparams(("parallel", "arbitrary", "arbitrary")),
        name="natten",
    )(proj, proj, proj, bias)


def _split_bf16(a):
    hi = a.astype(BF16)
    lo = (a - hi.astype(F32)).astype(BF16)
    return hi, lo


def _mix_kernel(x_ref, so_ref, qx_ref, kbd_ref, vbd_ref, obd_ref, wo_ref, g_ref, wrh_ref, wrl_ref,
                xo_ref, hn_ref, lg_ref):
    s = jnp.dot(qx_ref[...], kbd_ref[...], preferred_element_type=F32)
    ps = []
    for h in range(N_MEM_HEADS):
        s_h = s[:, h * N_MEM:(h + 1) * N_MEM]
        ps.append(jnp.exp2(s_h - jnp.max(s_h, axis=-1, keepdims=True)).astype(BF16))
    p = jnp.concatenate(ps, axis=-1)
    mem_out = (jnp.dot(p, vbd_ref[...], preferred_element_type=F32)
               / jnp.dot(p, obd_ref[...], preferred_element_type=F32)).astype(BF16)
    y = (x_ref[...]
         + jnp.dot(so_ref[...], wo_ref[:SELF_W, :], preferred_element_type=F32)
         + jnp.dot(mem_out, wo_ref[SELF_W:, :], preferred_element_type=F32))
    xo_ref[...] = y
    ms = jnp.mean(y * y, axis=-1, keepdims=True)
    hn = (y * lax.rsqrt(ms + EPS)) * g_ref[...]
    bits = pltpu.bitcast(hn.astype(BF16).astype(F32), jnp.uint32)
    half = hn.shape[1] // 2
    hn_ref[...] = (bits[:, :half] >> 16) | bits[:, half:]
    hi, lo = _split_bf16(hn)
    dn = (((1,), (1,)), ((), ()))
    lg = (lax.dot_general(wrh_ref[...], hi, dn, preferred_element_type=F32)
          + lax.dot_general(wrh_ref[...], lo, dn, preferred_element_type=F32)
          + lax.dot_general(wrl_ref[...], hi, dn, preferred_element_type=F32))
    for j in range(lg.shape[1] // LANES):
        lg_ref[:, j, :] = lg[:, j * LANES:(j + 1) * LANES]


def _mix(x2d, self_out2d, proj2d, qx_block, kv, wo_bf16, g_ffn, wr_t, seq, tm=1024):
    n, d = x2d.shape
    blocks_per_seq = seq // tm
    wrh, wrl = _split_bf16(wr_t)
    b = kv.shape[0]
    eye = jnp.eye(N_MEM_HEADS, dtype=kv.dtype)
    km = kv[..., :MEM_W].reshape(b, N_MEM, N_MEM_HEADS, HEAD_DIM)
    vm = kv[..., MEM_W:].reshape(b, N_MEM, N_MEM_HEADS, HEAD_DIM)
    kbd = (km.transpose(0, 2, 3, 1)[:, :, :, None, :] * eye[None, :, None, :, None]
           ).reshape(b, MEM_W, N_MEM_HEADS * N_MEM)
    vbd = (vm.transpose(0, 2, 1, 3)[:, :, :, None, :] * eye[None, :, None, :, None]
           ).reshape(b, N_MEM_HEADS * N_MEM, MEM_W)
    obd = jnp.asarray(np.kron(np.eye(N_MEM_HEADS), np.ones((N_MEM, HEAD_DIM))), BF16)
    return pl.pallas_call(
        _mix_kernel,
        grid=(n // tm,),
        in_specs=[
            pl.BlockSpec((tm, d), lambda i: (i, 0)),
            pl.BlockSpec((tm, SELF_W), lambda i: (i, 0)),
            pl.BlockSpec((tm, MEM_W), lambda i: (i, qx_block)),
            pl.BlockSpec((None, MEM_W, N_MEM_HEADS * N_MEM), lambda i: (i // blocks_per_seq, 0, 0)),
            pl.BlockSpec((None, N_MEM_HEADS * N_MEM, MEM_W), lambda i: (i // blocks_per_seq, 0, 0)),
            pl.BlockSpec((N_MEM_HEADS * N_MEM, MEM_W), lambda i: (0, 0)),
            pl.BlockSpec((d, d), lambda i: (0, 0)),
            pl.BlockSpec((1, d), lambda i: (0, 0)),
            pl.BlockSpec((N_EXPERTS, d), lambda i: (0, 0)),
            pl.BlockSpec((N_EXPERTS, d), lambda i: (0, 0)),
        ],
        out_specs=[
            pl.BlockSpec((tm, d), lambda i: (i, 0)),
            pl.BlockSpec((tm, d // 2), lambda i: (i, 0)),
            pl.BlockSpec((N_EXPERTS, tm // LANES, LANES), lambda i: (0, i, 0)),
        ],
        out_shape=[
            jax.ShapeDtypeStruct((n, d), F32),
            jax.ShapeDtypeStruct((n, d // 2), jnp.uint32),
            jax.ShapeDtypeStruct((N_EXPERTS, n // LANES, LANES), F32),
        ],
        compiler_params=_cparams(("parallel",)),
        name="mix_out_router",
    )(x2d, self_out2d, proj2d, kbd, vbd, obd, wo_bf16, g_ffn.reshape(1, d), wrh, wrl)


def _route_kernel(lg_ref, idx_ref, gate_ref, pos_ref, offs_ref, aff_s, gt_s, eq_s, need_s, *, cap):
    lg = lg_ref[...]
    n_e, nbk, _ = lg.shape
    ex = jnp.exp(lg - jnp.max(lg, axis=0, keepdims=True))
    aff = ex / jnp.sum(ex, axis=0, keepdims=True)
    keys = pltpu.bitcast(aff, jnp.int32)

    def count(mask_f):
        return jnp.sum(jnp.sum(mask_f, axis=1, keepdims=True), axis=2, keepdims=True)

    def bisect(it, thr):
        cand = thr | jnp.left_shift(jnp.int32(1), 30 - it)
        cnt = count(jnp.where(keys >= cand, 1.0, 0.0))
        return jnp.where(cnt >= float(cap), cand, thr)

    thr = lax.fori_loop(0, 31, bisect, jnp.zeros((n_e, 1, 1), jnp.int32))
    gt = jnp.where(keys > thr, 1.0, 0.0)
    eq = jnp.where(keys == thr, 1.0, 0.0)
    need = float(cap) - count(gt)
    aff_s[...] = aff
    gt_s[...] = gt
    eq_s[...] = eq
    need_s[...] = jnp.broadcast_to(need, need_s.shape)

    def tri(shape, fn):
        r = lax.broadcasted_iota(jnp.int32, shape, 0)
        c = lax.broadcasted_iota(jnp.int32, shape, 1)
        return jnp.where(fn(r, c), 1.0, 0.0).astype(BF16)

    u_lane = tri((LANES, LANES), lambda r, c: r <= c)
    l_blk = tri((nbk, nbk), lambda r, c: c < r)
    u_blk = tri((nbk, nbk), lambda r, c: r <= c)
    ones_l = jnp.ones((LANES, LANES), BF16)
    ones_b = jnp.ones((nbk, LANES), BF16)
    ones_8 = jnp.ones((8, LANES), BF16)
    dn_t = (((1,), (1,)), ((), ()))

    def cums(m):
        loc = jnp.dot(m.astype(BF16), u_lane, preferred_element_type=F32)
        tot = jnp.broadcast_to(loc[:, LANES - 1:LANES], (nbk, LANES))
        offs = jnp.dot(l_blk, tot.astype(BF16), preferred_element_type=F32)
        return loc, offs

    p_b = lax.broadcasted_iota(jnp.int32, (cap, nbk), 0).astype(F32)
    p_l = lax.broadcasted_iota(jnp.int32, (cap, LANES), 0).astype(F32)
    lane_b = lax.broadcasted_iota(jnp.int32, (cap, nbk), 1).astype(F32)
    lane_l = lax.broadcasted_iota(jnp.int32, (cap, LANES), 1).astype(F32)
    eye = (lax.broadcasted_iota(jnp.int32, (LANES, LANES), 0)
           == lax.broadcasted_iota(jnp.int32, (LANES, LANES), 1))

    def per_expert(e, _):
        eqm = eq_s[e]
        loc, offs = cums(eqm)
        tie_rank = loc + offs - eqm
        sel = gt_s[e] + eqm * jnp.where(tie_rank < need_s[e][0:1, 0:1], 1.0, 0.0)
        loc2, offs2 = cums(sel)
        pos_ref[e] = jnp.where(sel > 0.0, loc2 + offs2 - sel, -1.0).astype(jnp.int32)
        offs_ref[e] = offs2.astype(jnp.int32)
        tot_row = lax.dot_general(ones_8, sel.astype(BF16), dn_t, preferred_element_type=F32)
        s_row = jnp.dot(tot_row.astype(BF16), u_blk, preferred_element_type=F32)[0:1, :]
        nbv = jnp.dot(jnp.where(s_row <= p_b, 1.0, 0.0).astype(BF16), ones_b, preferred_element_type=F32)
        onehot = jnp.where(lane_b == nbv[:, :nbk], 1.0, 0.0).astype(BF16)
        g_loc = jnp.dot(onehot, loc2.astype(BF16), preferred_element_type=F32)
        o_hi = jnp.floor(offs2 * (1.0 / LANES))
        o_lo = offs2 - o_hi * LANES
        offp = (jnp.dot(onehot, o_hi.astype(BF16), preferred_element_type=F32) * LANES
                + jnp.dot(onehot, o_lo.astype(BF16), preferred_element_type=F32))
        il = jnp.dot(jnp.where(g_loc <= p_l - offp, 1.0, 0.0).astype(BF16), ones_l,
                     preferred_element_type=F32)
        tok = nbv * LANES + il
        a = aff_s[e]
        a1 = a.astype(BF16)
        r1 = a - a1.astype(F32)
        a2 = r1.astype(BF16)
        a3 = (r1 - a2.astype(F32)).astype(BF16)
        g_aff = (jnp.dot(onehot, a1, preferred_element_type=F32)
                 + jnp.dot(onehot, a2, preferred_element_type=F32)
                 + jnp.dot(onehot, a3, preferred_element_type=F32))
        gate_ref[e] = jnp.sum(jnp.where(lane_l == il, g_aff, 0.0), axis=1, keepdims=True)
        tok3 = tok.reshape(cap // LANES, LANES, LANES)
        idx_ref[e] = jnp.sum(jnp.where(eye[None], tok3, 0.0), axis=1).astype(jnp.int32)
        return 0

    lax.fori_loop(0, n_e, per_expert, 0)


def _route(logits_blk, cap):
    n_e, nbk, _ = logits_blk.shape
    return pl.pallas_call(
        functools.partial(_route_kernel, cap=cap),
        out_shape=[
            jax.ShapeDtypeStruct((n_e, cap // LANES, LANES), jnp.int32),
            jax.ShapeDtypeStruct((n_e, cap, 1), F32),
            jax.ShapeDtypeStruct((n_e, nbk, LANES), jnp.int32),
            jax.ShapeDtypeStruct((n_e, nbk, LANES), jnp.int32),
        ],
        scratch_shapes=[pltpu.VMEM((n_e, nbk, LANES), F32), pltpu.VMEM((n_e, nbk, LANES), F32),
                        pltpu.VMEM((n_e, nbk, LANES), F32), pltpu.VMEM((n_e, 8, LANES), F32)],
        compiler_params=pltpu.CompilerParams(vmem_limit_bytes=VMEM_LIMIT),
        name="route",
    )(logits_blk)


def _ffn_kernel(idx_ref, hn_hbm, gate_ref, wg_ref, wu_ref, wd_ref, o_ref, buf, sem, wgb, wub, wdb, *,
                chunk, cap, n_f):
    e = pl.program_id(0)
    f = pl.program_id(1)
    n_e = pl.num_programs(0)
    slot = e % 2
    half = wgb.shape[0] // 2

    def issue(expert, slot_, r0, count):
        def body(r, _):
            row = r0 + r
            tok = idx_ref[expert * cap + row]
            pltpu.make_async_copy(hn_hbm.at[pl.ds(tok, 1), :], buf.at[slot_, pl.ds(row, 1), :],
                                  sem.at[slot_]).start()
            return 0
        lax.fori_loop(0, count, body, 0, unroll=8)

    @pl.when((e == 0) & (f == 0))
    def _():
        issue(0, 0, 0, cap)

    @pl.when(f == 0)
    def _():
        pltpu.make_async_copy(buf.at[slot], buf.at[slot], sem.at[slot]).wait()
        o_ref[...] = jnp.zeros(o_ref.shape, F32)

    wgb[...] = wg_ref[...].astype(BF16)

    nxt = jnp.minimum(e + 1, n_e - 1)
    n_chunks = cap // chunk
    per_step = cap // n_f
    quota = -(-per_step // max(n_chunks - 1, 1))
    issue_at = [min(c * quota, per_step) for c in range(n_chunks)] + [per_step]
    if n_chunks > 1:
        issue_at[n_chunks - 1] = per_step
    last = f == n_f - 1

    for c in range(n_chunks):
        rs = slice(c * chunk, (c + 1) * chunk)
        w = buf[slot, rs, :]
        x_lo = pltpu.bitcast(w << 16, F32).astype(BF16)
        x_hi = pltpu.bitcast(w & jnp.uint32(0xFFFF0000), F32).astype(BF16)
        gg = (jnp.dot(x_lo, wgb[:half, :], preferred_element_type=F32)
              + jnp.dot(x_hi, wgb[half:, :], preferred_element_type=F32))
        for r in range(issue_at[c], issue_at[c + 1]):
            row = f * per_step + r
            tok = idx_ref[nxt * cap + row]
            pltpu.make_async_copy(hn_hbm.at[pl.ds(tok, 1), :], buf.at[1 - slot, pl.ds(row, 1), :],
                                  sem.at[1 - slot]).start()
        if c == 0:
            wub[...] = wu_ref[...].astype(BF16)
        uu = (jnp.dot(x_lo, wub[:half, :], preferred_element_type=F32)
              + jnp.dot(x_hi, wub[half:, :], preferred_element_type=F32))
        if c == 0:
            wdb[...] = wd_ref[...].astype(BF16)
        hid = (jax.nn.silu(gg) * uu).astype(BF16)
        part = jnp.dot(hid, wdb[...], preferred_element_type=F32)
        o_ref[rs, :] = (o_ref[rs, :] + part) * jnp.where(last, gate_ref[rs, :], 1.0)

    @pl.when((e == n_e - 1) & last)
    def _():
        pltpu.make_async_copy(buf.at[1 - slot], buf.at[1 - slot], sem.at[1 - slot]).wait()


def _expert_ffn(idx_flat, hn_packed, gates_col, w_gate, w_up, w_down, layer, cap, tf=512, chunk=512):
    d = 2 * hn_packed.shape[1]
    n_f = D_EXPERT // tf
    assert cap % (n_f * (cap // chunk)) == 0 and cap % chunk == 0
    kern = functools.partial(_ffn_kernel, chunk=chunk, cap=cap, n_f=n_f)
    grid_spec = pltpu.PrefetchScalarGridSpec(
        num_scalar_prefetch=1,
        grid=(N_EXPERTS, n_f),
        in_specs=[
            pl.BlockSpec(memory_space=pl.ANY),
            pl.BlockSpec((cap, 1), lambda e, f, idx: (e, 0)),
            pl.BlockSpec((None, None, d, tf), lambda e, f, idx: (layer, e, 0, f)),
            pl.BlockSpec((None, None, d, tf), lambda e, f, idx: (layer, e, 0, f)),
            pl.BlockSpec((None, None, tf, d), lambda e, f, idx: (layer, e, f, 0)),
        ],
        out_specs=pl.BlockSpec((cap, d), lambda e, f, idx: (e, 0)),
        scratch_shapes=[pltpu.VMEM((2, cap, d // 2), jnp.uint32), pltpu.SemaphoreType.DMA((2,)),
                        pltpu.VMEM((d, tf), BF16), pltpu.VMEM((d, tf), BF16), pltpu.VMEM((tf, d), BF16)],
    )
    return pl.pallas_call(
        kern,
        grid_spec=grid_spec,
        out_shape=jax.ShapeDtypeStruct((N_EXPERTS * cap, d), F32),
        compiler_params=_cparams(("arbitrary", "arbitrary")),
        name="expert_ffn",
    )(idx_flat, hn_packed, gates_col, w_gate, w_up, w_down)


SEG = 48
SEG_AL = SEG + 8


def _combine_kernel(offs_ref, y_ref, pos_ref, jl_ref, rmat_ref, ye_hbm, o_ref, stag, sem, *,
                    cap, nbk, blocks_per_tile):
    i = pl.program_id(0)
    n_t = pl.num_programs(0)
    slot = i % 2
    tile_rows = y_ref.shape[0]

    def seg(tile, e, k):
        base = offs_ref[e * (nbk + 1) + tile * blocks_per_tile] + k * SEG
        src = jnp.minimum((base // 8) * 8, cap - SEG_AL)
        return base, src

    def issue(tile, k, slot_):
        for e in range(N_EXPERTS):
            _, src = seg(tile, e, k)
            pltpu.make_async_copy(ye_hbm.at[pl.ds(pl.multiple_of(e * cap + src, 8), SEG_AL), :],
                                  stag.at[slot_, pl.ds(e * SEG_AL, SEG_AL), :], sem.at[slot_]).start()

    def wait(slot_):
        pltpu.make_async_copy(stag.at[slot_], stag.at[slot_], sem.at[slot_]).wait()

    def contrib(k):
        pos = pos_ref[...]
        lane_e = lax.broadcasted_iota(jnp.int32, pos.shape, 1)
        base_v = jnp.zeros(pos.shape, jnp.int32)
        src_v = jnp.zeros(pos.shape, jnp.int32)
        for e in range(N_EXPERTS):
            base, src = seg(i, e, k)
            base_v = jnp.where(lane_e == e, base, base_v)
            src_v = jnp.where(lane_e == e, src, src_v)
        valid = (pos >= base_v) & (pos < base_v + SEG)
        rel = jnp.where(valid, pos - src_v, -1).astype(F32).astype(BF16)
        rep = jnp.dot(rel, rmat_ref[...], preferred_element_type=F32)
        onehot = jnp.where(rep == jl_ref[...], 1.0, 0.0).astype(BF16)
        st = stag[slot]
        hi = st.astype(BF16)
        lo = (st - hi.astype(F32)).astype(BF16)
        return (jnp.dot(onehot, hi, preferred_element_type=F32)
                + jnp.dot(onehot, lo, preferred_element_type=F32))

    @pl.when(i == 0)
    def _():
        issue(0, 0, 0)

    @pl.when(i + 1 < n_t)
    def _():
        issue(i + 1, 0, 1 - slot)

    wait(slot)
    acc = y_ref[...] + contrib(0)

    max_cnt = jnp.int32(0)
    for e in range(N_EXPERTS):
        o0 = offs_ref[e * (nbk + 1) + i * blocks_per_tile]
        o1 = offs_ref[e * (nbk + 1) + (i + 1) * blocks_per_tile]
        max_cnt = jnp.maximum(max_cnt, o1 - o0)
    n_rounds = (max_cnt + SEG - 1) // SEG

    def extra(k, acc_):
        issue(i, k, slot)
        wait(slot)
        return acc_ + contrib(k)

    o_ref[...] = lax.fori_loop(1, n_rounds, extra, acc)


def _combine(offs_flat, y2d, pos_t, ye, cap, nbk, tile=256):
    n, d = y2d.shape
    ktot = N_EXPERTS * SEG_AL
    lane = np.arange(ktot)
    jl = jnp.asarray((lane % SEG_AL)[None, :], F32)
    rmat = jnp.asarray((lane[None, :] // SEG_AL) == np.arange(N_EXPERTS)[:, None], BF16)
    kern = functools.partial(_combine_kernel, cap=cap, nbk=nbk, blocks_per_tile=tile // LANES)
    grid_spec = pltpu.PrefetchScalarGridSpec(
        num_scalar_prefetch=1,
        grid=(n // tile,),
        in_specs=[
            pl.BlockSpec((tile, d), lambda i, offs: (i, 0)),
            pl.BlockSpec((tile, N_EXPERTS), lambda i, offs: (i, 0)),
            pl.BlockSpec((1, ktot), lambda i, offs: (0, 0)),
            pl.BlockSpec((N_EXPERTS, ktot), lambda i, offs: (0, 0)),
            pl.BlockSpec(memory_space=pl.ANY),
        ],
        out_specs=pl.BlockSpec((tile, d), lambda i, offs: (i, 0)),
        scratch_shapes=[pltpu.VMEM((2, ktot, d), F32), pltpu.SemaphoreType.DMA((2,))],
    )
    return pl.pallas_call(
        kern,
        grid_spec=grid_spec,
        out_shape=jax.ShapeDtypeStruct((n, d), F32),
        compiler_params=_cparams(("arbitrary",)),
        name="combine",
    )(offs_flat, y2d, pos_t, jl, rmat, ye)


def _tile_gain(g64, n_heads, scale=1.0):
    return jnp.tile(g64.astype(F32) * scale, n_heads)


def _prepare(p):
    q = dict(p)
    for name in ("w_in_a", "w_in_b", "w_mem_kv", "w_o"):
        q[name] = p[name].astype(BF16)
    q["na_bias"] = [_natten_bias(p["na_rpb"][j]) for j in range(p["na_rpb"].shape[0])]
    q["prepared"] = True
    return q


def _trunk(x, mem, p):
    if "prepared" not in p:
        p = _prepare(p)
    batch, seq, d = x.shape
    n = batch * seq
    cap = EC_CAPACITY_FACTOR * n // N_EXPERTS
    cos_t, sin_t = _rope_tables(seq)
    cos_m, sin_m = cos_t[:N_MEM], sin_t[:N_MEM]
    qscale = HEAD_DIM ** -0.5 * LOG2E
    x2d = x.reshape(n, d)
    mem2d = mem.reshape(batch * N_MEM, d)
    ones_kv = jnp.ones((MEM_W,), F32)
    for i in range(DEPTH):
        j = i // 2
        xq_gain = _tile_gain(p["xq_norm"][i], N_MEM_HEADS, qscale)
        if i % 2 == 0:
            gain = jnp.concatenate([_tile_gain(p["q_norm_a"][j], N_SELF_HEADS, qscale),
                                    _tile_gain(p["k_norm_a"][j], N_KV_HEADS),
                                    jnp.ones((KV_W_A,), F32), xq_gain])
            mask = [True] * 8 + [False] * 2 + [True] * 2
            proj = _fused_proj(x2d, p["norm_mix"][i], p["w_in_a"][j], gain, mask, 8,
                               cos_t, sin_t, seq, tm=512)
            self_out = _gqa_attention(proj.reshape(batch, seq, IN_W_A), batch, seq)
            qx_block = (SELF_W + 2 * KV_W_A) // MEM_W
        else:
            gain = jnp.concatenate([_tile_gain(p["q_norm_b"][j], N_SELF_HEADS, qscale),
                                    _tile_gain(p["k_norm_b"][j], N_SELF_HEADS),
                                    jnp.ones((SELF_W,), F32), xq_gain])
            mask = [True] * 12 + [False] * 6 + [True] * 2
            proj = _fused_proj(x2d, p["norm_mix"][i], p["w_in_b"][j], gain, mask, 0,
                               cos_t, sin_t, seq, tm=512)
            self_out = _natten(proj.reshape(batch, seq, IN_W_B), p["na_bias"][j], batch, seq)
            qx_block = 3 * SELF_W // MEM_W
        kv_gain = jnp.concatenate([_tile_gain(p["xk_norm"][i], N_MEM_HEADS), ones_kv])
        kv = _fused_proj(mem2d, p["norm_mem"][i], p["w_mem_kv"][i], kv_gain,
                         [True, True, False, False], 0, cos_m, sin_m, N_MEM)
        x2d, hn, logits_t = _mix(x2d, self_out.reshape(n, SELF_W), proj, qx_block,
                                 kv.reshape(batch, N_MEM, 2 * MEM_W), p["w_o"][i],
                                 p["norm_ffn"][i], p["w_router"][i].T, seq)
        nbk = n // LANES
        idx, gates, pos, offs = _route(logits_t, cap)
        ye = _expert_ffn(idx.reshape(-1), hn, gates.reshape(-1, 1),
                         p["w_gate"], p["w_up"], p["w_down"], i, cap)
        offs_flat = jnp.concatenate([offs[:, :, 0], jnp.full((N_EXPERTS, 1), cap, jnp.int32)],
                                    axis=1).reshape(-1)
        x2d = _combine(offs_flat, x2d, pos.reshape(N_EXPERTS, n).T, ye, cap, nbk)
    return x2d.reshape(batch, seq, d)


def kernel(x_prompt, x_sample, mem_prompt, mem_sample, norm_mix, w_in_a, q_norm_a, k_norm_a, w_in_b, q_norm_b, k_norm_b, na_rpb, norm_mem, w_mem_kv, xq_norm, xk_norm, w_o, norm_ffn, w_router, w_gate, w_up, w_down):
    p = dict(norm_mix=norm_mix, w_in_a=w_in_a, q_norm_a=q_norm_a, k_norm_a=k_norm_a, w_in_b=w_in_b,
             q_norm_b=q_norm_b, k_norm_b=k_norm_b, na_rpb=na_rpb, norm_mem=norm_mem, w_mem_kv=w_mem_kv,
             xq_norm=xq_norm, xk_norm=xk_norm, w_o=w_o, norm_ffn=norm_ffn, w_router=w_router,
             w_gate=w_gate, w_up=w_up, w_down=w_down)
    p = _prepare(p)
    return _trunk(x_prompt, mem_prompt, p), _trunk(x_sample, mem_sample, p)
```
